```python
import math
import jax, jax.numpy as jnp
from jax import lax
import numpy as np

D_MODEL = 2048
BATCH = 2
SEQ = 16384
DEPTH = 2

CHUNK = 64
Q_BLOCK = 128
HEAD_DIM = 128
MIX_WIDTH = D_MODEL
N_MIX_HEADS = MIX_WIDTH // HEAD_DIM
A_HEADS = (3 * N_MIX_HEADS) // 8
B_HEADS = (N_MIX_HEADS - A_HEADS) // 2
C_HEADS = N_MIX_HEADS - A_HEADS - B_HEADS
A_QK_DIM = HEAD_DIM // 2
A_WIDTH = A_HEADS * HEAD_DIM
B_WIDTH = B_HEADS * HEAD_DIM
C_WIDTH = C_HEADS * HEAD_DIM
A_QK_COLS = A_HEADS * 2 * A_QK_DIM
IN_SPLITS = (A_QK_COLS, A_QK_COLS, A_WIDTH,
             B_WIDTH, B_WIDTH, B_WIDTH, B_WIDTH, B_HEADS,
             C_WIDTH, C_WIDTH, C_WIDTH)
IN_COLS = sum(IN_SPLITS)
C_LEFT_CHUNKS = 8
C_BAND = (C_LEFT_CHUNKS + 1) * CHUNK
REL_CLIP = 256
MEM_LEN = 256
MEM_HEADS = 4
MEM_WIDTH = MEM_HEADS * HEAD_DIM
D_FF = 5632
ROPE_THETA = 10000.0
EPS = 1e-6
NEG = -1e30

kernel_name = "hybrid_streaming_parallel_groups"


def rmsnorm(x, g):
    xf = x.astype(jnp.float32)
    y = xf * lax.rsqrt(jnp.mean(xf * xf, axis=-1, keepdims=True) + EPS)
    return (y * g.astype(jnp.float32)).astype(x.dtype)


def rope(x, positions):
    d = x.shape[-1]
    half = d // 2
    inv_freq = jnp.power(jnp.float32(ROPE_THETA), -jnp.arange(half, dtype=jnp.float32) / half)
    ang = positions.astype(jnp.float32)[..., None] * inv_freq
    cos = jnp.cos(ang)[:, :, None, :]
    sin = jnp.sin(ang)[:, :, None, :]
    xf = x.astype(jnp.float32)
    x1, x2 = xf[..., :half], xf[..., half:]
    out = jnp.concatenate([x1 * cos - x2 * sin, x2 * cos + x1 * sin], axis=-1)
    return out.astype(x.dtype)


def swiglu(u, w_gate, w_up, w_down):
    g = jnp.einsum('bsd,df->bsf', u, w_gate)
    v = jnp.einsum('bsd,df->bsf', u, w_up)
    return jnp.einsum('bsf,fd->bsd', jax.nn.silu(g) * v, w_down)


def diff_attention(q, k, v, lam):
    B, S, H, _, dq = q.shape
    nb = S // Q_BLOCK
    scale = dq ** -0.5
    qb = jnp.swapaxes(q.reshape(B, nb, Q_BLOCK, H, 2, dq), 0, 1)
    key_chunk = jnp.arange(S) // CHUNK

    def block(args):
        qi, bi = args
        s = jnp.einsum('bqhcd,bkhcd->bchqk', qi, k,
                       preferred_element_type=jnp.float32) * scale
        q_chunk = (bi * Q_BLOCK + jnp.arange(Q_BLOCK)) // CHUNK
        mask = key_chunk[None, :] <= q_chunk[:, None]
        p = jax.nn.softmax(jnp.where(mask, s, NEG), axis=-1)
        w = p[:, 0] - lam * p[:, 1]
        return jnp.einsum('bhqk,bkhd->bqhd', w.astype(v.dtype), v)

    o = lax.map(block, (qb, jnp.arange(nb)))
    return jnp.swapaxes(o, 0, 1).reshape(B, S, H, v.shape[-1])


def forgetting_attention(q, k, v, log_f):
    B, S, H, d = q.shape
    nb = S // Q_BLOCK
    scale = d ** -0.5
    F = jnp.cumsum(log_f, axis=1)
    Fk = jnp.transpose(F, (0, 2, 1))
    qb = jnp.swapaxes(q.reshape(B, nb, Q_BLOCK, H, d), 0, 1)
    Fqb = jnp.transpose(Fk.reshape(B, H, nb, Q_BLOCK), (2, 0, 1, 3))
    kpos = jnp.arange(S)

    def block(args):
        qi, Fq, bi = args
        s = jnp.einsum('bqhd,bkhd->bhqk', qi, k, preferred_element_type=jnp.float32) * scale
        s = s + Fq[..., None] - Fk[:, :, None, :]
        qpos = bi * Q_BLOCK + jnp.arange(Q_BLOCK)
        p = jax.nn.softmax(jnp.where(kpos[None, :] <= qpos[:, None], s, NEG), axis=-1)
        return jnp.einsum('bhqk,bkhd->bqhd', p.astype(v.dtype), v)

    o = lax.map(block, (qb, Fqb, jnp.arange(nb)))
    return jnp.swapaxes(o, 0, 1).reshape(B, S, H, d)


def chunk_band_attention(q, k, v, rel_bias):
    B, S, H, d = q.shape
    NC = S // CHUNK
    scale = d ** -0.5
    qc = q.reshape(B, NC, CHUNK, H, d)
    pad = ((0, 0), (C_LEFT_CHUNKS, 0), (0, 0), (0, 0), (0, 0))
    kp = jnp.pad(k.reshape(B, NC, CHUNK, H, d), pad)
    vp = jnp.pad(v.reshape(B, NC, CHUNK, H, d), pad)
    band_idx = jnp.arange(NC)[:, None] + jnp.arange(C_LEFT_CHUNKS + 1)[None, :]
    kb = kp[:, band_idx].reshape(B, NC, C_BAND, H, d)
    vb = vp[:, band_idx].reshape(B, NC, C_BAND, H, d)
    s = jnp.einsum('bnqhd,bnkhd->bnhqk', qc, kb, preferred_element_type=jnp.float32) * scale
    dist = C_LEFT_CHUNKS * CHUNK + jnp.arange(CHUNK)[:, None] - jnp.arange(C_BAND)[None, :]
    idx = jnp.clip(dist, -REL_CLIP, REL_CLIP) + REL_CLIP
    bias = rel_bias[:, idx].astype(jnp.float32)
    valid = (jnp.arange(NC)[:, None] - C_LEFT_CHUNKS
             + (jnp.arange(C_BAND) // CHUNK)[None, :]) >= 0
    s = jnp.where(valid[None, :, None, None, :], s + bias[None, None], NEG)
    p = jax.nn.softmax(s, axis=-1)
    o = jnp.einsum('bnhqk,bnkhd->bnqhd', p.astype(v.dtype), vb)
    return o.reshape(B, S, H, d)


def token_mixing(u, positions, w_in, w_out, lam_q1, lam_k1, lam_q2, lam_k2,
                 diff_subln_g, fox_forget_b, chunk_rel_bias, lam_init):
    B, S, _ = u.shape
    h = jnp.einsum('bsd,dc->bsc', u, w_in)
    cuts = np.cumsum(np.array(IN_SPLITS))[:-1].tolist()
    aq, ak, av, bq, bk, bv, bg, bf, cq, ck, cv = jnp.split(h, cuts, axis=-1)

    aq = rope(aq.reshape(B, S, A_HEADS * 2, A_QK_DIM), positions).reshape(B, S, A_HEADS, 2, A_QK_DIM)
    ak = rope(ak.reshape(B, S, A_HEADS * 2, A_QK_DIM), positions).reshape(B, S, A_HEADS, 2, A_QK_DIM)
    av = av.reshape(B, S, A_HEADS, HEAD_DIM)
    lam = (jnp.exp(jnp.sum(lam_q1.astype(jnp.float32) * lam_k1.astype(jnp.float32)))
           - jnp.exp(jnp.sum(lam_q2.astype(jnp.float32) * lam_k2.astype(jnp.float32)))
           + lam_init)
    oa = diff_attention(aq, ak, av, lam)
    oa = rmsnorm(oa, diff_subln_g) * (1.0 - lam_init)

    log_f = jax.nn.log_sigmoid(bf.astype(jnp.float32) + fox_forget_b.astype(jnp.float32))
    ob = forgetting_attention(bq.reshape(B, S, B_HEADS, HEAD_DIM),
                              bk.reshape(B, S, B_HEADS, HEAD_DIM),
                              bv.reshape(B, S, B_HEADS, HEAD_DIM), log_f)
    ob = ob.reshape(B, S, B_WIDTH) * jax.nn.sigmoid(bg)

    oc = chunk_band_attention(cq.reshape(B, S, C_HEADS, HEAD_DIM),
                              ck.reshape(B, S, C_HEADS, HEAD_DIM),
                              cv.reshape(B, S, C_HEADS, HEAD_DIM), chunk_rel_bias)

    o = jnp.concatenate([oa.reshape(B, S, A_WIDTH), ob, oc.reshape(B, S, C_WIDTH)], axis=-1)
    return jnp.einsum('bsc,cd->bsd', o, w_out)


def memory_attention(u, mem_n, w_q, w_kv, w_o):
    B, S, _ = u.shape
    M = mem_n.shape[1]
    q = jnp.einsum('bsd,dc->bsc', u, w_q).reshape(B, S, MEM_HEADS, HEAD_DIM)
    kv = jnp.einsum('bmd,dc->bmc', mem_n, w_kv)
    k = kv[..., :MEM_WIDTH].reshape(B, M, MEM_HEADS, HEAD_DIM)
    v = kv[..., MEM_WIDTH:].reshape(B, M, MEM_HEADS, HEAD_DIM)
    s = jnp.einsum('bshd,bmhd->bhsm', q, k, preferred_element_type=jnp.float32) * HEAD_DIM ** -0.5
    p = jax.nn.softmax(s, axis=-1)
    o = jnp.einsum('bhsm,bmhd->bshd', p.astype(v.dtype), v).reshape(B, S, MEM_WIDTH)
    return jnp.einsum('bsc,cd->bsd', o, w_o)


def setup_inputs(seed: int = 0) -> dict:
    key = jax.random.key(seed)
    ks = iter(jax.random.split(key, 40))
    L, D = DEPTH, D_MODEL

    def nrm(shape, scale):
        return jax.random.normal(next(ks), shape, jnp.float32) * scale

    def gain(shape):
        return 1.0 + nrm(shape, 0.05)

    x = nrm((BATCH, SEQ, D), 1.0)
    mem = nrm((BATCH, MEM_LEN, D), 1.0)
    offset = jax.random.randint(next(ks), (BATCH, 1), 0, 4096, dtype=jnp.int32)
    positions = (offset + jnp.arange(SEQ, dtype=jnp.int32)[None, :]).astype(jnp.int32)
    return {
        "x": x, "mem": mem, "positions": positions,
        "ffn1_pre_g": gain((L, D)), "ffn1_post_g": gain((L, D)),
        "ffn1_w_gate": nrm((L, D, D_FF), D ** -0.5),
        "ffn1_w_up": nrm((L, D, D_FF), D ** -0.5),
        "ffn1_w_down": nrm((L, D_FF, D), D_FF ** -0.5),
        "mix_pre_g": gain((L, D)), "mix_post_g": gain((L, D)),
        "w_in": nrm((L, D, IN_COLS), D ** -0.5),
        "w_out": nrm((L, MIX_WIDTH, D), MIX_WIDTH ** -0.5),
        "lam_q1": nrm((L, A_QK_DIM), 0.1), "lam_k1": nrm((L, A_QK_DIM), 0.1),
        "lam_q2": nrm((L, A_QK_DIM), 0.1), "lam_k2": nrm((L, A_QK_DIM), 0.1),
        "diff_subln_g": gain((L, HEAD_DIM)),
        "fox_forget_b": 2.0 + nrm((L, B_HEADS), 0.5),
        "chunk_rel_bias": nrm((L, C_HEADS, 2 * REL_CLIP + 1), 0.1),
        "mem_pre_g": gain((L, D)), "mem_post_g": gain((L, D)), "mem_kv_g": gain((L, D)),
        "w_mem_q": nrm((L, D, MEM_WIDTH), D ** -0.5),
        "w_mem_kv": nrm((L, D, 2 * MEM_WIDTH), D ** -0.5),
        "w_mem_o": nrm((L, MEM_WIDTH, D), MEM_WIDTH ** -0.5),
        "ffn2_pre_g": gain((L, D)), "ffn2_post_g": gain((L, D)),
        "ffn2_w_gate": nrm((L, D, D_FF), D ** -0.5),
        "ffn2_w_up": nrm((L, D, D_FF), D ** -0.5),
        "ffn2_w_down": nrm((L, D_FF, D), D_FF ** -0.5),
    }


def reference(x, mem, positions,
              ffn1_pre_g, ffn1_post_g, ffn1_w_gate, ffn1_w_up, ffn1_w_down,
              mix_pre_g, mix_post_g, w_in, w_out,
              lam_q1, lam_k1, lam_q2, lam_k2, diff_subln_g,
              fox_forget_b, chunk_rel_bias,
              mem_pre_g, mem_post_g, mem_kv_g, w_mem_q, w_mem_kv, w_mem_o,
              ffn2_pre_g, ffn2_post_g, ffn2_w_gate, ffn2_w_up, ffn2_w_down):
    for l in range(DEPTH):
        lam_init = 0.8 - 0.6 * math.exp(-0.3 * l)
        y = swiglu(rmsnorm(x, ffn1_pre_g[l]), ffn1_w_gate[l], ffn1_w_up[l], ffn1_w_down[l])
        x = x + 0.5 * rmsnorm(y, ffn1_post_g[l])
        y = token_mixing(rmsnorm(x, mix_pre_g[l]), positions, w_in[l], w_out[l],
                         lam_q1[l], lam_k1[l], lam_q2[l], lam_k2[l], diff_subln_g[l],
                         fox_forget_b[l], chunk_rel_bias[l], lam_init)
        x = x + rmsnorm(y, mix_post_g[l])
        y = memory_attention(rmsnorm(x, mem_pre_g[l]), rmsnorm(mem, mem_kv_g[l]),
                             w_mem_q[l], w_mem_kv[l], w_mem_o[l])
        x = x + rmsnorm(y, mem_post_g[l])
        y = swiglu(rmsnorm(x, ffn2_pre_g[l]), ffn2_w_gate[l], ffn2_w_up[l], ffn2_w_down[l])
        x = x + 0.5 * rmsnorm(y, ffn2_post_g[l])
    return x
```

```python
import functools
import math

import numpy as np
import jax
import jax.numpy as jnp
from jax import lax
from jax.experimental import pallas as pl
from jax.experimental.pallas import tpu as pltpu

F32 = jnp.float32
BF16 = jnp.bfloat16

CHUNK = 64
HEAD_DIM = 128
A_HEADS, B_HEADS, C_HEADS = 6, 5, 5
A_QK_DIM = HEAD_DIM // 2
A_WIDTH, B_WIDTH, C_WIDTH = A_HEADS * HEAD_DIM, B_HEADS * HEAD_DIM, C_HEADS * HEAD_DIM
C_LEFT_CHUNKS = 8
REL_CLIP = 256
MEM_HEADS = 4
MEM_WIDTH = MEM_HEADS * HEAD_DIM
ROPE_THETA = 10000.0
EPS = 1e-6
NEG = -1e30
LOG2E = math.log2(math.e)

LANES = 128
V7X_VMEM_BYTES = 64 * 2**20
VMEM_COMPILER_RESERVE = 8 * 2**20

FFN_ROWS, FFN_COLS = 512, 512
PROJ_ROWS = 256
ATTN_TILE = 512
BAND_ROWS = 256
BAND_KEYS = BAND_ROWS + C_LEFT_CHUNKS * CHUNK
BAND_VARIANTS = BAND_KEYS // BAND_ROWS
TOEPLITZ = 1024
OUT_ROWS = 512

_NT = (((1,), (1,)), ((), ()))


def _nbytes(shape, dtype):
    return math.prod(shape) * jnp.dtype(dtype).itemsize


def _params(semantics, buffers, temp_bytes=0):
    need = sum(_nbytes(s, d) * n for s, d, n in buffers) + temp_bytes
    limit = min(need + VMEM_COMPILER_RESERVE, V7X_VMEM_BYTES - VMEM_COMPILER_RESERVE // 2)
    return pltpu.CompilerParams(dimension_semantics=semantics, vmem_limit_bytes=int(limit))


def _resident(shape):
    n = len(shape)
    return pl.BlockSpec(shape, lambda *_: (0,) * n, pipeline_mode=pl.Buffered(1))


def _rms(x, g):
    return x * lax.rsqrt(jnp.mean(x * x, axis=-1, keepdims=True) + EPS) * g


def _lane_tile(a, n):
    return a if n == 1 else jnp.concatenate([a] * n, axis=1)


def _rope_table_kernel(pos_ref, invf_ref, sign_ref, cos_ref, sin_ref):
    ang = pos_ref[...].astype(F32) * invf_ref[...]
    cos_ref[...] = jnp.cos(ang)
    sin_ref[...] = jnp.sin(ang) * sign_ref[...]


def _rope_table(positions):
    n = positions.size
    rows = min(2048, n)
    half = A_QK_DIM // 2
    inv_freq = jnp.power(jnp.float32(ROPE_THETA), -jnp.arange(half, dtype=F32) / half)
    invf = jnp.tile(inv_freq, LANES // half)[None, :]
    sign = jnp.tile(jnp.concatenate([-jnp.ones((half,), F32), jnp.ones((half,), F32)]),
                    LANES // (2 * half))[None, :]
    row = pl.BlockSpec((rows, LANES), lambda i: (i, 0))
    return pl.pallas_call(
        _rope_table_kernel,
        grid=(n // rows,),
        in_specs=[pl.BlockSpec((rows, 1), lambda i: (i, 0)), _resident((1, LANES)), _resident((1, LANES))],
        out_specs=[row, row],
        out_shape=[jax.ShapeDtypeStruct((n, LANES), F32)] * 2,
        compiler_params=_params(("parallel",), [((rows, LANES), F32, 6)], 8 * rows * LANES * 4),
        name="rope_table",
    )(positions.reshape(n, 1), invf, sign)


def _ffn_kernel(x_ref, pre_ref, post_ref, wg_ref, wu_ref, wd_ref, o_ref, u_ref):
    f = pl.program_id(1)

    @pl.when(f == 0)
    def _():
        u_ref[...] = _rms(x_ref[...], pre_ref[...]).astype(BF16)
        o_ref[...] = jnp.zeros_like(o_ref)

    u = u_ref[...]
    g = jnp.dot(u, wg_ref[...], preferred_element_type=F32)
    v = jnp.dot(u, wu_ref[...], preferred_element_type=F32)
    h = (g * jax.nn.sigmoid(g) * v).astype(BF16)
    o_ref[...] += jnp.dot(h, wd_ref[...], preferred_element_type=F32)

    @pl.when(f == pl.num_programs(1) - 1)
    def _():
        o_ref[...] = x_ref[...] + 0.5 * _rms(o_ref[...], post_ref[...])


def _ffn(x, pre_g, post_g, w_gate, w_up, w_down):
    n, d = x.shape
    dff = w_gate.shape[1]
    tm, tf = min(FFN_ROWS, n), FFN_COLS
    row = pl.BlockSpec((tm, d), lambda i, f: (i, 0))
    return pl.pallas_call(
        _ffn_kernel,
        grid=(n // tm, dff // tf),
        in_specs=[row, _resident((1, d)), _resident((1, d)),
                  pl.BlockSpec((d, tf), lambda i, f: (0, f)),
                  pl.BlockSpec((d, tf), lambda i, f: (0, f)),
                  pl.BlockSpec((tf, d), lambda i, f: (f, 0))],
        out_specs=row,
        out_shape=jax.ShapeDtypeStruct((n, d), F32),
        scratch_shapes=[pltpu.VMEM((tm, d), BF16)],
        compiler_params=_params(("parallel", "arbitrary"),
                                [((tm, d), F32, 4), ((tm, d), BF16, 1), ((d, tf), BF16, 6)],
                                4 * tm * tf * 4),
        name="ffn",
    )(x, pre_g[None, :], post_g[None, :], w_gate.astype(BF16), w_up.astype(BF16), w_down.astype(BF16))


def _inproj_kernel(x_ref, g_ref, cos_ref, sin_ref,
                   waq, wak, wav, wbq, wbk, wbv, wbg, wbf, bbf, wcq, wck, wcv,
                   aq, ak, av, bq, bk, bv, bgate, fcol, frow, cq, ck, cv,
                   carry, *, tiles_per_seq):
    i = pl.program_id(0)
    tm = x_ref.shape[0]
    u = _rms(x_ref[...], g_ref[...]).astype(BF16)
    cos, sin = cos_ref[...], sin_ref[...]
    lane = lax.broadcasted_iota(jnp.int32, (tm, LANES), 1)
    first_half = (lane & (A_QK_DIM // 2)) == 0

    def proj(w_ref):
        return jnp.dot(u, w_ref[...], preferred_element_type=F32)

    def rope_store(w_ref, o_ref, scale):
        a = proj(w_ref)
        for h in range(A_HEADS):
            t = a[:, h * LANES:(h + 1) * LANES]
            partner = jnp.where(first_half, pltpu.roll(t, LANES - A_QK_DIM // 2, 1),
                                pltpu.roll(t, A_QK_DIM // 2, 1))
            o_ref[:, h * LANES:(h + 1) * LANES] = ((t * cos + partner * sin) * scale).astype(BF16)

    rope_store(waq, aq, A_QK_DIM ** -0.5 * LOG2E)
    rope_store(wak, ak, 1.0)
    av[...] = proj(wav).astype(BF16)
    bq[...] = (proj(wbq) * (HEAD_DIM ** -0.5 * LOG2E)).astype(BF16)
    bk[...] = proj(wbk).astype(BF16)
    bv[...] = proj(wbv).astype(BF16)
    bgate[...] = jax.nn.sigmoid(proj(wbg)).astype(BF16)
    cq[...] = (proj(wcq) * (HEAD_DIM ** -0.5 * LOG2E)).astype(BF16)
    ck[...] = proj(wck).astype(BF16)
    cv[...] = proj(wcv).astype(BF16)

    z = proj(wbf) + bbf[...]
    logf = jnp.minimum(z, 0.0) - jnp.log1p(jnp.exp(-jnp.abs(z)))

    @pl.when(i % tiles_per_seq == 0)
    def _():
        carry[...] = jnp.zeros_like(carry)

    r = lax.broadcasted_iota(jnp.int32, (tm, tm), 0)
    c = lax.broadcasted_iota(jnp.int32, (tm, tm), 1)
    tri = (r >= c).astype(F32)
    cum = jnp.dot(tri, logf, precision=lax.Precision.HIGHEST, preferred_element_type=F32) + carry[...]
    carry[...] = cum[tm - 1:tm, :]
    cum2 = cum * LOG2E
    fcol[...] = cum2
    frow[...] = cum2.T


def _inproj(x, g, cos, sin, w_in, forget_b, seq):
    n, d = x.shape
    tm = min(PROJ_ROWS, seq)
    widths = (A_WIDTH, A_WIDTH, A_WIDTH, B_WIDTH, B_WIDTH, B_WIDTH, B_WIDTH, B_HEADS, C_WIDTH, C_WIDTH, C_WIDTH)
    cuts = np.cumsum((0,) + widths)
    parts = [w_in[:, a:b].astype(BF16) for a, b in zip(cuts[:-1], cuts[1:])]
    waq, wak, wav, wbq, wbk, wbv, wbg, wbf, wcq, wck, wcv = parts
    wbf = jnp.pad(wbf, ((0, 0), (0, LANES - B_HEADS)))
    bbf = jnp.pad(forget_b.astype(F32), (0, LANES - B_HEADS))[None, :]
    weights = (waq, wak, wav, wbq, wbk, wbv, wbg, wbf, bbf, wcq, wck, wcv)

    def rows(width):
        return pl.BlockSpec((tm, width), lambda i: (i, 0))

    out_widths = (A_WIDTH,) * 3 + (B_WIDTH,) * 4 + (LANES,) + (C_WIDTH,) * 3
    out_specs = [rows(A_WIDTH)] * 3 + [rows(B_WIDTH)] * 4 + [rows(LANES), pl.BlockSpec((LANES, tm), lambda i: (0, i))] \
        + [rows(C_WIDTH)] * 3
    out_shape = [jax.ShapeDtypeStruct((n, A_WIDTH), BF16)] * 3 + [jax.ShapeDtypeStruct((n, B_WIDTH), BF16)] * 4 \
        + [jax.ShapeDtypeStruct((n, LANES), F32), jax.ShapeDtypeStruct((LANES, n), F32)] \
        + [jax.ShapeDtypeStruct((n, C_WIDTH), BF16)] * 3
    buffers = [((tm, d), F32, 2), ((tm, LANES), F32, 8)] + [(w.shape, w.dtype, 1) for w in weights] \
        + [((tm, w), BF16, 2) for w in out_widths]
    return pl.pallas_call(
        functools.partial(_inproj_kernel, tiles_per_seq=seq // tm),
        grid=(n // tm,),
        in_specs=[rows(d), _resident((1, d)), rows(LANES), rows(LANES)] + [_resident(w.shape) for w in weights],
        out_specs=out_specs,
        out_shape=out_shape,
        scratch_shapes=[pltpu.VMEM((1, LANES), F32)],
        compiler_params=_params(("arbitrary",), buffers, 6 * tm * A_WIDTH * 4 + tm * d * 6),
        name="inproj",
    )(x, g[None, :], cos, sin, *weights)


def _online_softmax_step(q, k, v_ext, m_ref, acc_ref, bias=None, mask=None):
    s = lax.dot_general(q, k, _NT, preferred_element_type=F32)
    if bias is not None:
        s = s + bias
    if mask is not None:
        s = jnp.where(mask, s, NEG)
    m_prev = m_ref[...]
    m_new = jnp.maximum(m_prev, jnp.max(s, axis=-1, keepdims=True))
    alpha = jnp.exp2(m_prev - m_new)
    p = jnp.exp2(s - _lane_tile(m_new, s.shape[1] // LANES))
    pv = jnp.dot(p.astype(BF16), v_ext, preferred_element_type=F32)
    acc_ref[...] = _lane_tile(alpha, 2) * acc_ref[...] + pv
    m_ref[...] = m_new


def _init_softmax_state(*refs):
    for m_ref, acc_ref in zip(refs[::2], refs[1::2]):
        m_ref[...] = jnp.full(m_ref.shape, NEG, F32)
        acc_ref[...] = jnp.zeros_like(acc_ref)


def _normalized(acc_ref):
    return acc_ref[:, :LANES] / acc_ref[:, LANES:]


def _diff_attn_kernel(q_ref, k_ref, v_ref, lam_ref, g_ref, o_ref, m1, acc1, m2, acc2, *, lam_init):
    i = pl.program_id(2)
    t = q_ref.shape[0]
    q = q_ref[...].astype(F32)
    lane = lax.broadcasted_iota(jnp.int32, q.shape, 1)
    q1 = jnp.where(lane < A_QK_DIM, q, 0.0).astype(BF16)
    q2 = jnp.where(lane >= A_QK_DIM, q, 0.0).astype(BF16)
    ones = jnp.ones((t, LANES), BF16)
    _init_softmax_state(m1, acc1, m2, acc2)

    def step(j, mask):
        start = pl.multiple_of(j * t, t)
        k = k_ref[pl.ds(start, t), :]
        v_ext = jnp.concatenate([v_ref[pl.ds(start, t), :], ones], axis=1)
        _online_softmax_step(q1, k, v_ext, m1, acc1, mask=mask)
        _online_softmax_step(q2, k, v_ext, m2, acc2, mask=mask)

    def full_tile(j, carry):
        step(j, None)
        return carry

    lax.fori_loop(0, i, full_tile, 0)
    row = lax.broadcasted_iota(jnp.int32, (t, t), 0)
    col = lax.broadcasted_iota(jnp.int32, (t, t), 1)
    step(i, (col // CHUNK) <= (row // CHUNK))

    lp = lam_ref[...]
    lam = (jnp.exp(jnp.sum(lp[0:1] * lp[1:2], axis=-1, keepdims=True))
           - jnp.exp(jnp.sum(lp[2:3] * lp[3:4], axis=-1, keepdims=True)) + lam_init)
    o = _normalized(acc1) - lam * _normalized(acc2)
    o_ref[...] = (_rms(o, g_ref[...]) * (1.0 - lam_init)).astype(BF16)


def _diff_attention(aq, ak, av, lam_params, subln_g, batch, seq, lam_init):
    t = min(ATTN_TILE, seq)
    nq = seq // t
    qo = pl.BlockSpec((t, LANES), lambda b, h, i: (b * nq + i, h))
    kv = pl.BlockSpec((seq, LANES), lambda b, h, i: (b, h))
    return pl.pallas_call(
        functools.partial(_diff_attn_kernel, lam_init=lam_init),
        grid=(batch, A_HEADS, nq),
        in_specs=[qo, kv, kv, _resident((4, LANES)), _resident((1, LANES))],
        out_specs=qo,
        out_shape=jax.ShapeDtypeStruct(aq.shape, BF16),
        scratch_shapes=[pltpu.VMEM((t, LANES), F32), pltpu.VMEM((t, 2 * LANES), F32)] * 2,
        compiler_params=_params(("parallel", "parallel", "arbitrary"),
                                [((t, LANES), BF16, 4), ((seq, LANES), BF16, 4), ((t, LANES), F32, 6)],
                                6 * t * t * 4),
        name="diff_attn",
    )(aq, ak, av, lam_params, subln_g[None, :])


def _fox_attn_kernel(q_ref, k_ref, v_ref, gate_ref, fcol_ref, frow_ref, o_ref, m, acc):
    h = pl.program_id(1)
    i = pl.program_id(2)
    t = q_ref.shape[0]
    q = q_ref[...]
    lane = lax.broadcasted_iota(jnp.int32, (t, LANES), 1)
    fq = jnp.sum(jnp.where(lane == h, fcol_ref[...], 0.0), axis=-1, keepdims=True)
    ones = jnp.ones((t, LANES), BF16)
    _init_softmax_state(m, acc)

    def step(j, mask):
        start = pl.multiple_of(j * t, t)
        k = k_ref[pl.ds(start, t), :]
        v_ext = jnp.concatenate([v_ref[pl.ds(start, t), :], ones], axis=1)
        bias = fq - frow_ref[:, pl.ds(start, t)]
        _online_softmax_step(q, k, v_ext, m, acc, bias=bias, mask=mask)

    def full_tile(j, carry):
        step(j, None)
        return carry

    lax.fori_loop(0, i, full_tile, 0)
    row = lax.broadcasted_iota(jnp.int32, (t, t), 0)
    col = lax.broadcasted_iota(jnp.int32, (t, t), 1)
    step(i, col <= row)
    o_ref[...] = (_normalized(acc) * gate_ref[...].astype(F32)).astype(BF16)


def _fox_attention(bq, bk, bv, bgate, fcol, frow, batch, seq):
    t = min(ATTN_TILE, seq)
    nq = seq // t
    qo = pl.BlockSpec((t, LANES), lambda b, h, i: (b * nq + i, h))
    kv = pl.BlockSpec((seq, LANES), lambda b, h, i: (b, h))
    return pl.pallas_call(
        _fox_attn_kernel,
        grid=(batch, B_HEADS, nq),
        in_specs=[qo, kv, kv, qo,
                  pl.BlockSpec((t, LANES), lambda b, h, i: (b * nq + i, 0)),
                  pl.BlockSpec((None, None, 1, seq), lambda b, h, i: (h, b, 0, 0))],
        out_specs=qo,
        out_shape=jax.ShapeDtypeStruct(bq.shape, BF16),
        scratch_shapes=[pltpu.VMEM((t, LANES), F32), pltpu.VMEM((t, 2 * LANES), F32)],
        compiler_params=_params(("parallel", "parallel", "arbitrary"),
                                [((t, LANES), BF16, 6), ((seq, LANES), BF16, 4), ((t, LANES), F32, 5),
                                 ((8, seq), F32, 2)],
                                6 * t * t * 4),
        name="fox_attn",
    )(bq, bk, bv, bgate, fcol, frow)


def _band_bias_kernel(table_ref, o_ref):
    row = lax.broadcasted_iota(jnp.int32, (BAND_ROWS, TOEPLITZ), 0)
    qc = lax.broadcasted_iota(jnp.int32, (BAND_ROWS, BAND_KEYS), 0) // CHUNK
    kc = lax.broadcasted_iota(jnp.int32, (BAND_ROWS, BAND_KEYS), 1) // CHUNK
    for v in range(BAND_VARIANTS):
        x = jnp.broadcast_to(table_ref[v], (BAND_ROWS, TOEPLITZ))
        shift = 1
        while shift < BAND_ROWS:
            x = jnp.where((row & shift) != 0, pltpu.roll(x, shift, 1), x)
            shift *= 2
        q_chunk = qc + v * (BAND_ROWS // CHUNK)
        visible = (kc <= q_chunk) & (kc >= q_chunk - C_LEFT_CHUNKS)
        o_ref[v] = jnp.where(visible, x[:, :BAND_KEYS], NEG)


def _band_bias(rel_bias):
    j = np.arange(TOEPLITZ)
    key_minus_q = np.where(j < BAND_KEYS, j, j - TOEPLITZ)
    idx = np.stack([np.clip(v * BAND_ROWS - key_minus_q, -REL_CLIP, REL_CLIP) + REL_CLIP
                    for v in range(BAND_VARIANTS)])
    table = (rel_bias.astype(F32) * LOG2E)[:, idx][:, :, None, :]
    return pl.pallas_call(
        _band_bias_kernel,
        grid=(C_HEADS,),
        in_specs=[pl.BlockSpec((None, BAND_VARIANTS, 1, TOEPLITZ), lambda h: (h, 0, 0, 0))],
        out_specs=pl.BlockSpec((None, BAND_VARIANTS, BAND_ROWS, BAND_KEYS), lambda h: (h, 0, 0, 0)),
        out_shape=jax.ShapeDtypeStruct((C_HEADS, BAND_VARIANTS, BAND_ROWS, BAND_KEYS), F32),
        compiler_params=_params(("parallel",), [((BAND_VARIANTS, BAND_ROWS, BAND_KEYS), F32, 2)],
                                6 * BAND_ROWS * TOEPLITZ * 4),
        name="band_bias",
    )(table)


def _band_attn_kernel(q_ref, k_ref, v_ref, bias_ref, o_ref):
    i = pl.program_id(2)
    first = jnp.maximum(i - (BAND_VARIANTS - 1), 0)
    start = pl.multiple_of(first * BAND_ROWS, BAND_ROWS)
    k = k_ref[pl.ds(start, BAND_KEYS), :]
    v = v_ref[pl.ds(start, BAND_KEYS), :]
    s = lax.dot_general(q_ref[...], k, _NT, preferred_element_type=F32) + bias_ref[i - first]
    p = jnp.exp2(s - jnp.max(s, axis=-1, keepdims=True))
    denom = jnp.sum(p, axis=-1, keepdims=True)
    o = jnp.dot(p.astype(BF16), v, preferred_element_type=F32)
    o_ref[...] = (o / denom).astype(BF16)


def _band_attention(cq, ck, cv, bias, batch, seq):
    nq = seq // BAND_ROWS
    qo = pl.BlockSpec((BAND_ROWS, LANES), lambda b, h, i: (b * nq + i, h))
    kv = pl.BlockSpec((seq, LANES), lambda b, h, i: (b, h))
    return pl.pallas_call(
        _band_attn_kernel,
        grid=(batch, C_HEADS, nq),
        in_specs=[qo, kv, kv,
                  pl.BlockSpec((None, BAND_VARIANTS, BAND_ROWS, BAND_KEYS), lambda b, h, i: (h, 0, 0, 0))],
        out_specs=qo,
        out_shape=jax.ShapeDtypeStruct(cq.shape, BF16),
        compiler_params=_params(("parallel", "parallel", "arbitrary"),
                                [((BAND_ROWS, LANES), BF16, 4), ((seq, LANES), BF16, 4),
                                 ((BAND_VARIANTS, BAND_ROWS, BAND_KEYS), F32, 2)],
                                4 * BAND_ROWS * BAND_KEYS * 4),
        name="band_attn",
    )(cq, ck, cv, bias)


def _outproj_kernel(oa_ref, ob_ref, oc_ref, x_ref, g_ref, wa_ref, wb_ref, wc_ref, o_ref):
    y = jnp.dot(oa_ref[...], wa_ref[...], preferred_element_type=F32)
    y += jnp.dot(ob_ref[...], wb_ref[...], preferred_element_type=F32)
    y += jnp.dot(oc_ref[...], wc_ref[...], preferred_element_type=F32)
    o_ref[...] = x_ref[...] + _rms(y, g_ref[...])


def _outproj(oa, ob, oc, x, g, w_out):
    n, d = x.shape
    tm = min(OUT_ROWS, n)
    w = w_out.astype(BF16)
    wa, wb, wc = w[:A_WIDTH], w[A_WIDTH:A_WIDTH + B_WIDTH], w[A_WIDTH + B_WIDTH:]

    def rows(width):
        return pl.BlockSpec((tm, width), lambda i: (i, 0))

    return pl.pallas_call(
        _outproj_kernel,
        grid=(n // tm,),
        in_specs=[rows(A_WIDTH), rows(B_WIDTH), rows(C_WIDTH), rows(d), _resident((1, d)),
                  _resident(wa.shape), _resident(wb.shape), _resident(wc.shape)],
        out_specs=rows(d),
        out_shape=jax.ShapeDtypeStruct((n, d), F32),
        compiler_params=_params(("parallel",),
                                [((tm, d), F32, 4), ((tm, d), BF16, 2), (w.shape, BF16, 1)], 2 * tm * d * 4),
        name="outproj",
    )(oa, ob, oc, x, g[None, :], wa, wb, wc)


def _mem_kv_kernel(mem_ref, g_ref, w_ref, o_ref):
    u = _rms(mem_ref[...], g_ref[...]).astype(BF16)
    o_ref[...] = jnp.dot(u, w_ref[...], preferred_element_type=F32).astype(BF16)


def _mem_kv(mem2d, g, w_kv, mem_len):
    n, d = mem2d.shape
    w = w_kv.astype(BF16)
    return pl.pallas_call(
        _mem_kv_kernel,
        grid=(n // mem_len,),
        in_specs=[pl.BlockSpec((mem_len, d), lambda b: (b, 0)), _resident((1, d)), _resident(w.shape)],
        out_specs=pl.BlockSpec((mem_len, w.shape[1]), lambda b: (b, 0)),
        out_shape=jax.ShapeDtypeStruct((n, w.shape[1]), BF16),
        compiler_params=_params(("parallel",), [((mem_len, d), F32, 3), (w.shape, BF16, 1)], mem_len * d * 8),
        name="mem_kv",
    )(mem2d, g[None, :], w)


def _mem_attn_kernel(x_ref, pre_ref, post_ref, wq_ref, kv_ref, wo_ref, o_ref):
    x = x_ref[...]
    u = _rms(x, pre_ref[...]).astype(BF16)
    q = (jnp.dot(u, wq_ref[...], preferred_element_type=F32) * (HEAD_DIM ** -0.5 * LOG2E)).astype(BF16)
    heads = []
    for h in range(MEM_HEADS):
        qh = q[:, h * LANES:(h + 1) * LANES]
        kh = kv_ref[:, h * LANES:(h + 1) * LANES]
        vh = kv_ref[:, MEM_WIDTH + h * LANES:MEM_WIDTH + (h + 1) * LANES]
        s = lax.dot_general(qh, kh, _NT, preferred_element_type=F32)
        p = jnp.exp2(s - jnp.max(s, axis=-1, keepdims=True))
        denom = jnp.sum(p, axis=-1, keepdims=True)
        heads.append((jnp.dot(p.astype(BF16), vh, preferred_element_type=F32) / denom).astype(BF16))
    y = jnp.dot(jnp.concatenate(heads, axis=1), wo_ref[...], preferred_element_type=F32)
    o_ref[...] = x + _rms(y, post_ref[...])


def _mem_attention(x, pre_g, post_g, w_q, kv, w_o, seq, mem_len):
    n, d = x.shape
    tm = min(OUT_ROWS, seq)
    tiles_per_seq = seq // tm
    wq, wo = w_q.astype(BF16), w_o.astype(BF16)
    row = pl.BlockSpec((tm, d), lambda i: (i, 0))
    return pl.pallas_call(
        _mem_attn_kernel,
        grid=(n // tm,),
        in_specs=[row, _resident((1, d)), _resident((1, d)), _resident(wq.shape),
                  pl.BlockSpec((mem_len, kv.shape[1]), lambda i: (i // tiles_per_seq, 0)),
                  _resident(wo.shape)],
        out_specs=row,
        out_shape=jax.ShapeDtypeStruct((n, d), F32),
        compiler_params=_params(("parallel",),
                                [((tm, d), F32, 4), (wq.shape, BF16, 1), (wo.shape, BF16, 1),
                                 ((mem_len, kv.shape[1]), BF16, 2)], 3 * tm * d * 4),
        name="mem_attn",
    )(x, pre_g[None, :], post_g[None, :], wq, kv, wo)


def kernel(x, mem, positions, ffn1_pre_g, ffn1_post_g, ffn1_w_gate, ffn1_w_up, ffn1_w_down, mix_pre_g, mix_post_g, w_in, w_out, lam_q1, lam_k1, lam_q2, lam_k2, diff_subln_g, fox_forget_b, chunk_rel_bias, mem_pre_g, mem_post_g, mem_kv_g, w_mem_q, w_mem_kv, w_mem_o, ffn2_pre_g, ffn2_post_g, ffn2_w_gate, ffn2_w_up, ffn2_w_down):
    batch, seq, d = x.shape
    mem_len = mem.shape[1]
    depth = w_in.shape[0]
    assert seq % ATTN_TILE == 0 or seq < ATTN_TILE, seq
    assert seq % BAND_ROWS == 0 and seq >= BAND_KEYS, seq

    h = x.reshape(batch * seq, d)
    mem2d = mem.reshape(batch * mem_len, d)
    cos, sin = _rope_table(positions)
    for l in range(depth):
        lam_init = 0.8 - 0.6 * math.exp(-0.3 * l)
        h = _ffn(h, ffn1_pre_g[l], ffn1_post_g[l], ffn1_w_gate[l], ffn1_w_up[l], ffn1_w_down[l])

        aq, ak, av, bq, bk, bv, bgate, fcol, frow, cq, ck, cv = _inproj(
            h, mix_pre_g[l], cos, sin, w_in[l], fox_forget_b[l], seq)
        lam_params = jnp.pad(jnp.stack([lam_q1[l], lam_k1[l], lam_q2[l], lam_k2[l]]).astype(F32),
                             ((0, 0), (0, LANES - A_QK_DIM)))
        oa = _diff_attention(aq, ak, av, lam_params, diff_subln_g[l], batch, seq, lam_init)
        frow_heads = frow[:B_HEADS].reshape(B_HEADS, batch, 1, seq)
        ob = _fox_attention(bq, bk, bv, bgate, fcol, frow_heads, batch, seq)
        oc = _band_attention(cq, ck, cv, _band_bias(chunk_rel_bias[l]), batch, seq)
        h = _outproj(oa, ob, oc, h, mix_post_g[l], w_out[l])

        kv = _mem_kv(mem2d, mem_kv_g[l], w_mem_kv[l], mem_len)
        h = _mem_attention(h, mem_pre_g[l], mem_post_g[l], w_mem_q[l], kv, w_mem_o[l], seq, mem_len)

        h = _ffn(h, ffn2_pre_g[l], ffn2_post_g[l], ffn2_w_gate[l], ffn2_w_up[l], ffn2_w_down[l])
    return h.reshape(batch, seq, d)
```

```python
import functools
import math
from typing import NamedTuple

import numpy as np
import jax
import jax.numpy as jnp
from jax import lax
from jax.experimental import pallas as pl
from jax.experimental.pallas import tpu as pltpu

F32 = jnp.float32
BF16 = jnp.bfloat16

CHUNK = 64
HEAD_DIM = 128
A_HEADS, B_HEADS, C_HEADS = 6, 5, 5
A_QK_DIM = HEAD_DIM // 2
A_WIDTH, B_WIDTH, C_WIDTH = A_HEADS * HEAD_DIM, B_HEADS * HEAD_DIM, C_HEADS * HEAD_DIM
C_LEFT_CHUNKS = 8
REL_CLIP = 256
MEM_HEADS = 4
MEM_WIDTH = MEM_HEADS * HEAD_DIM
ROPE_THETA = 10000.0
EPS = 1e-6
NEG = -1e30
LOG2E = math.log2(math.e)

LANES = 128
V7X_VMEM_BYTES = 64 * 2**20
VMEM_COMPILER_RESERVE = 8 * 2**20

FFN_ROWS, FFN_COLS = 512, 512
PROJ_ROWS = 256
ATTN_TILE = 512
BAND_ROWS = 256
BAND_KEYS = BAND_ROWS + C_LEFT_CHUNKS * CHUNK
BAND_VARIANTS = BAND_KEYS // BAND_ROWS
TOEPLITZ = 1024
OUT_ROWS = 512

_NT = (((1,), (1,)), ((), ()))


def _nbytes(shape, dtype):
    return math.prod(shape) * jnp.dtype(dtype).itemsize


def _params(semantics, buffers, temp_bytes=0):
    need = sum(_nbytes(s, d) * n for s, d, n in buffers) + temp_bytes
    limit = min(need + VMEM_COMPILER_RESERVE, V7X_VMEM_BYTES - VMEM_COMPILER_RESERVE // 2)
    return pltpu.CompilerParams(dimension_semantics=semantics, vmem_limit_bytes=int(limit))


def _resident(shape):
    n = len(shape)
    return pl.BlockSpec(shape, lambda *_: (0,) * n, pipeline_mode=pl.Buffered(1))


def _rms(x, g):
    return x * lax.rsqrt(jnp.mean(x * x, axis=-1, keepdims=True) + EPS) * g


def _lane_tile(a, n):
    return a if n == 1 else jnp.concatenate([a] * n, axis=1)


def _rope_table_kernel(pos_ref, invf_ref, sign_ref, cos_ref, sin_ref):
    ang = pos_ref[...].astype(F32) * invf_ref[...]
    cos_ref[...] = jnp.cos(ang)
    sin_ref[...] = jnp.sin(ang) * sign_ref[...]


def _rope_table(positions):
    n = positions.size
    rows = math.gcd(n, 2048)
    half = A_QK_DIM // 2
    inv_freq = jnp.power(jnp.float32(ROPE_THETA), -jnp.arange(half, dtype=F32) / half)
    invf = jnp.tile(inv_freq, LANES // half)[None, :]
    sign = jnp.tile(jnp.concatenate([-jnp.ones((half,), F32), jnp.ones((half,), F32)]),
                    LANES // (2 * half))[None, :]
    row = pl.BlockSpec((rows, LANES), lambda i: (i, 0))
    return pl.pallas_call(
        _rope_table_kernel,
        grid=(n // rows,),
        in_specs=[pl.BlockSpec((rows, 1), lambda i: (i, 0)), _resident((1, LANES)), _resident((1, LANES))],
        out_specs=[row, row],
        out_shape=[jax.ShapeDtypeStruct((n, LANES), F32)] * 2,
        compiler_params=_params(("parallel",), [((rows, LANES), F32, 6)], 8 * rows * LANES * 4),
        name="rope_table",
    )(positions.reshape(n, 1), invf, sign)


def _ffn_kernel(x_ref, pre_ref, post_ref, wg_ref, wu_ref, wd_ref, o_ref, u_ref):
    f = pl.program_id(1)

    @pl.when(f == 0)
    def _():
        u_ref[...] = _rms(x_ref[...], pre_ref[...]).astype(BF16)
        o_ref[...] = jnp.zeros_like(o_ref)

    u = u_ref[...]
    g = jnp.dot(u, wg_ref[...], preferred_element_type=F32)
    v = jnp.dot(u, wu_ref[...], preferred_element_type=F32)
    h = (g * jax.nn.sigmoid(g) * v).astype(BF16)
    o_ref[...] += jnp.dot(h, wd_ref[...], preferred_element_type=F32)

    @pl.when(f == pl.num_programs(1) - 1)
    def _():
        o_ref[...] = x_ref[...] + 0.5 * _rms(o_ref[...], post_ref[...])


def _ffn(x, pre_g, post_g, w_gate, w_up, w_down):
    n, d = x.shape
    dff = w_gate.shape[1]
    tm, tf = min(FFN_ROWS, n), FFN_COLS
    row = pl.BlockSpec((tm, d), lambda i, f: (i, 0))
    return pl.pallas_call(
        _ffn_kernel,
        grid=(n // tm, dff // tf),
        in_specs=[row, _resident((1, d)), _resident((1, d)),
                  pl.BlockSpec((d, tf), lambda i, f: (0, f)),
                  pl.BlockSpec((d, tf), lambda i, f: (0, f)),
                  pl.BlockSpec((tf, d), lambda i, f: (f, 0))],
        out_specs=row,
        out_shape=jax.ShapeDtypeStruct((n, d), F32),
        scratch_shapes=[pltpu.VMEM((tm, d), BF16)],
        compiler_params=_params(("parallel", "arbitrary"),
                                [((tm, d), F32, 4), ((tm, d), BF16, 1), ((d, tf), BF16, 6)],
                                4 * tm * tf * 4),
        name="ffn",
    )(x, pre_g[None, :], post_g[None, :], w_gate.astype(BF16), w_up.astype(BF16), w_down.astype(BF16))


def _inproj_kernel(x_ref, g_ref, cos_ref, sin_ref,
                   waq, wak, wav, wbq, wbk, wbv, wbg, wbf, bbf, wcq, wck, wcv,
                   aq, ak, av, bq, bk, bv, bgate, fcol, frow, cq, ck, cv,
                   carry, *, tiles_per_seq):
    i = pl.program_id(0)
    tm = x_ref.shape[0]
    u = _rms(x_ref[...], g_ref[...]).astype(BF16)
    cos, sin = cos_ref[...], sin_ref[...]
    lane = lax.broadcasted_iota(jnp.int32, (tm, LANES), 1)
    first_half = (lane & (A_QK_DIM // 2)) == 0

    def proj(w_ref):
        return jnp.dot(u, w_ref[...], preferred_element_type=F32)

    def rope_store(w_ref, o_ref, scale):
        a = proj(w_ref)
        for h in range(A_HEADS):
            t = a[:, h * LANES:(h + 1) * LANES]
            partner = jnp.where(first_half, pltpu.roll(t, LANES - A_QK_DIM // 2, 1),
                                pltpu.roll(t, A_QK_DIM // 2, 1))
            o_ref[:, h * LANES:(h + 1) * LANES] = ((t * cos + partner * sin) * scale).astype(BF16)

    rope_store(waq, aq, A_QK_DIM ** -0.5 * LOG2E)
    rope_store(wak, ak, 1.0)
    av[...] = proj(wav).astype(BF16)
    bq[...] = (proj(wbq) * (HEAD_DIM ** -0.5 * LOG2E)).astype(BF16)
    bk[...] = proj(wbk).astype(BF16)
    bv[...] = proj(wbv).astype(BF16)
    bgate[...] = jax.nn.sigmoid(proj(wbg)).astype(BF16)
    cq[...] = (proj(wcq) * (HEAD_DIM ** -0.5 * LOG2E)).astype(BF16)
    ck[...] = proj(wck).astype(BF16)
    cv[...] = proj(wcv).astype(BF16)

    z = proj(wbf) + bbf[...]
    logf = jnp.minimum(z, 0.0) - jnp.log1p(jnp.exp(-jnp.abs(z)))

    @pl.when(i % tiles_per_seq == 0)
    def _():
        carry[...] = jnp.zeros_like(carry)

    r = lax.broadcasted_iota(jnp.int32, (tm, tm), 0)
    c = lax.broadcasted_iota(jnp.int32, (tm, tm), 1)
    tri = (r >= c).astype(F32)
    cum = jnp.dot(tri, logf, precision=lax.Precision.HIGHEST, preferred_element_type=F32) + carry[...]
    carry[...] = cum[tm - 1:tm, :]
    cum2 = cum * LOG2E
    fcol[...] = cum2
    frow[...] = cum2.T


def _inproj(x, g, cos, sin, w_in, forget_b, seq):
    n, d = x.shape
    tm = min(PROJ_ROWS, seq)
    widths = (A_WIDTH, A_WIDTH, A_WIDTH, B_WIDTH, B_WIDTH, B_WIDTH, B_WIDTH, B_HEADS, C_WIDTH, C_WIDTH, C_WIDTH)
    cuts = np.cumsum((0,) + widths)
    parts = [w_in[:, a:b].astype(BF16) for a, b in zip(cuts[:-1], cuts[1:])]
    waq, wak, wav, wbq, wbk, wbv, wbg, wbf, wcq, wck, wcv = parts
    wbf = jnp.pad(wbf, ((0, 0), (0, LANES - B_HEADS)))
    bbf = jnp.pad(forget_b.astype(F32), (0, LANES - B_HEADS))[None, :]
    weights = (waq, wak, wav, wbq, wbk, wbv, wbg, wbf, bbf, wcq, wck, wcv)

    def rows(width):
        return pl.BlockSpec((tm, width), lambda i: (i, 0))

    out_widths = (A_WIDTH,) * 3 + (B_WIDTH,) * 4 + (LANES,) + (C_WIDTH,) * 3
    out_specs = [rows(A_WIDTH)] * 3 + [rows(B_WIDTH)] * 4 + [rows(LANES), pl.BlockSpec((LANES, tm), lambda i: (0, i))] \
        + [rows(C_WIDTH)] * 3
    out_shape = [jax.ShapeDtypeStruct((n, A_WIDTH), BF16)] * 3 + [jax.ShapeDtypeStruct((n, B_WIDTH), BF16)] * 4 \
        + [jax.ShapeDtypeStruct((n, LANES), F32), jax.ShapeDtypeStruct((LANES, n), F32)] \
        + [jax.ShapeDtypeStruct((n, C_WIDTH), BF16)] * 3
    buffers = [((tm, d), F32, 2), ((tm, LANES), F32, 8)] + [(w.shape, w.dtype, 1) for w in weights] \
        + [((tm, w), BF16, 2) for w in out_widths]
    return pl.pallas_call(
        functools.partial(_inproj_kernel, tiles_per_seq=seq // tm),
        grid=(n // tm,),
        in_specs=[rows(d), _resident((1, d)), rows(LANES), rows(LANES)] + [_resident(w.shape) for w in weights],
        out_specs=out_specs,
        out_shape=out_shape,
        scratch_shapes=[pltpu.VMEM((1, LANES), F32)],
        compiler_params=_params(("arbitrary",), buffers, 6 * tm * A_WIDTH * 4 + tm * d * 6),
        name="inproj",
    )(x, g[None, :], cos, sin, *weights)


class _Stream(NamedTuple):
    q: jax.Array
    m_ref: object
    acc_ref: object
    s_refs: tuple
    p_refs: tuple
    a_refs: tuple


def _causal_sweep(i, t, k_ref, v_ref, streams, bias_fn, diag_mask):
    ones = jnp.ones((t, LANES), BF16)

    def scores(j):
        start = pl.multiple_of(j * t, t)
        k = k_ref[pl.ds(start, t), :]
        bias = None if bias_fn is None else bias_fn(start)
        out = []
        for st in streams:
            s = lax.dot_general(st.q, k, _NT, preferred_element_type=F32)
            out.append(s if bias is None else s + bias)
        return tuple(out)

    def softmax(ss, mask):
        out = []
        for st, s in zip(streams, ss):
            if mask is not None:
                s = jnp.where(mask, s, NEG)
            m_prev = st.m_ref[...]
            m_new = jnp.maximum(m_prev, jnp.max(s, axis=-1, keepdims=True))
            st.m_ref[...] = m_new
            out.append((jnp.exp2(s - _lane_tile(m_new, t // LANES)).astype(BF16), jnp.exp2(m_prev - m_new)))
        return tuple(out)

    def accumulate(j, weights):
        start = pl.multiple_of(j * t, t)
        v_ext = jnp.concatenate([v_ref[pl.ds(start, t), :], ones], axis=1)
        for st, (p, alpha) in zip(streams, weights):
            pv = jnp.dot(p, v_ext, preferred_element_type=F32)
            st.acc_ref[...] = _lane_tile(alpha, 2) * st.acc_ref[...] + pv

    for st in streams:
        st.m_ref[...] = jnp.full(st.m_ref.shape, NEG, F32)
        st.acc_ref[...] = jnp.zeros_like(st.acc_ref)

    @pl.when(i < 2)
    def _():
        def full_tile(j, carry):
            accumulate(j, softmax(scores(j), None))
            return carry

        lax.fori_loop(0, i, full_tile, 0)
        accumulate(i, softmax(scores(i), diag_mask))

    def put_scores(j, par):
        for st, s in zip(streams, scores(j)):
            st.s_refs[par][...] = s

    def take_scores(par):
        return tuple(st.s_refs[par][...] for st in streams)

    def put_weights(par, weights):
        for st, (p, alpha) in zip(streams, weights):
            st.p_refs[par][...] = p
            st.a_refs[par][...] = alpha

    def take_weights(par):
        return tuple((st.p_refs[par][...], st.a_refs[par][...]) for st in streams)

    def steady(j, par):
        accumulate(j - 2, take_weights(par))
        put_weights(1 - par, softmax(take_scores(1 - par), None))
        put_scores(j, par)

    def finish(par):
        accumulate(i - 1, take_weights(1 - par))
        accumulate(i, softmax(take_scores(par), diag_mask))

    @pl.when(i >= 2)
    def _():
        put_scores(0, 0)
        put_weights(0, softmax(take_scores(0), None))
        put_scores(1, 1)

        def steady_pair(n, carry):
            steady(2 + 2 * n, 0)

            @pl.when(n >= 0)
            def _():
                steady(3 + 2 * n, 1)

            return carry

        lax.fori_loop(0, lax.shift_right_logical(i - 1, 1), steady_pair, 0)

        @pl.when((i & 1) == 0)
        def _():
            steady(i, 0)
            finish(0)

        @pl.when((i & 1) == 1)
        def _():
            finish(1)


def _sweep_scratch(t, n_streams):
    per_stream = [pltpu.VMEM((t, LANES), F32), pltpu.VMEM((t, 2 * LANES), F32)] \
        + [pltpu.VMEM((t, t), F32)] * 2 + [pltpu.VMEM((t, t), BF16)] * 2 + [pltpu.VMEM((t, LANES), F32)] * 2
    return per_stream * n_streams


def _stream(q, scratch):
    m_ref, acc_ref, s0, s1, p0, p1, a0, a1 = scratch
    return _Stream(q, m_ref, acc_ref, (s0, s1), (p0, p1), (a0, a1))


_SCRATCH_PER_STREAM = 8


def _sweep_buffers(t, n_streams):
    return [((t, LANES), F32, 5 * n_streams), ((t, t), F32, 2 * n_streams), ((t, t), BF16, 2 * n_streams)]


def _normalized(acc_ref):
    return acc_ref[:, :LANES] / acc_ref[:, LANES:]


def _diff_attn_kernel(q_ref, k_ref, v_ref, lam_ref, g_ref, o_ref, *scratch, lam_init):
    i = pl.program_id(2)
    t = q_ref.shape[0]
    q = q_ref[...].astype(F32)
    lane = lax.broadcasted_iota(jnp.int32, q.shape, 1)
    q1 = jnp.where(lane < A_QK_DIM, q, 0.0).astype(BF16)
    q2 = jnp.where(lane >= A_QK_DIM, q, 0.0).astype(BF16)
    st1 = _stream(q1, scratch[:_SCRATCH_PER_STREAM])
    st2 = _stream(q2, scratch[_SCRATCH_PER_STREAM:])
    row = lax.broadcasted_iota(jnp.int32, (t, t), 0)
    col = lax.broadcasted_iota(jnp.int32, (t, t), 1)
    _causal_sweep(i, t, k_ref, v_ref, (st1, st2), None, (col // CHUNK) <= (row // CHUNK))

    lp = lam_ref[...]
    lam = (jnp.exp(jnp.sum(lp[0:1] * lp[1:2], axis=-1, keepdims=True))
           - jnp.exp(jnp.sum(lp[2:3] * lp[3:4], axis=-1, keepdims=True)) + lam_init)
    o = _normalized(st1.acc_ref) - lam * _normalized(st2.acc_ref)
    o_ref[...] = (_rms(o, g_ref[...]) * (1.0 - lam_init)).astype(BF16)


def _diff_attention(aq, ak, av, lam_params, subln_g, batch, seq, lam_init):
    t = min(ATTN_TILE, seq)
    nq = seq // t
    qo = pl.BlockSpec((t, LANES), lambda b, h, i: (b * nq + i, h))
    kv = pl.BlockSpec((seq, LANES), lambda b, h, i: (b, h))
    return pl.pallas_call(
        functools.partial(_diff_attn_kernel, lam_init=lam_init),
        grid=(batch, A_HEADS, nq),
        in_specs=[qo, kv, kv, _resident((4, LANES)), _resident((1, LANES))],
        out_specs=qo,
        out_shape=jax.ShapeDtypeStruct(aq.shape, BF16),
        scratch_shapes=_sweep_scratch(t, 2),
        compiler_params=_params(("parallel", "parallel", "arbitrary"),
                                [((t, LANES), BF16, 4), ((seq, LANES), BF16, 4)] + _sweep_buffers(t, 2),
                                6 * t * t * 4),
        name="diff_attn",
    )(aq, ak, av, lam_params, subln_g[None, :])


def _fox_attn_kernel(q_ref, k_ref, v_ref, gate_ref, fcol_ref, frow_ref, o_ref, *scratch):
    h = pl.program_id(1)
    i = pl.program_id(2)
    t = q_ref.shape[0]
    lane = lax.broadcasted_iota(jnp.int32, (t, LANES), 1)
    fq = jnp.sum(jnp.where(lane == h, fcol_ref[...], 0.0), axis=-1, keepdims=True)
    st = _stream(q_ref[...], scratch)
    row = lax.broadcasted_iota(jnp.int32, (t, t), 0)
    col = lax.broadcasted_iota(jnp.int32, (t, t), 1)

    def forget_bias(start):
        return fq - frow_ref[:, pl.ds(start, t)]

    _causal_sweep(i, t, k_ref, v_ref, (st,), forget_bias, col <= row)
    o_ref[...] = (_normalized(st.acc_ref) * gate_ref[...].astype(F32)).astype(BF16)


def _fox_attention(bq, bk, bv, bgate, fcol, frow, batch, seq):
    t = min(ATTN_TILE, seq)
    nq = seq // t
    qo = pl.BlockSpec((t, LANES), lambda b, h, i: (b * nq + i, h))
    kv = pl.BlockSpec((seq, LANES), lambda b, h, i: (b, h))
    return pl.pallas_call(
        _fox_attn_kernel,
        grid=(batch, B_HEADS, nq),
        in_specs=[qo, kv, kv, qo,
                  pl.BlockSpec((t, LANES), lambda b, h, i: (b * nq + i, 0)),
                  pl.BlockSpec((None, None, 1, seq), lambda b, h, i: (h, b, 0, 0))],
        out_specs=qo,
        out_shape=jax.ShapeDtypeStruct(bq.shape, BF16),
        scratch_shapes=_sweep_scratch(t, 1),
        compiler_params=_params(("parallel", "parallel", "arbitrary"),
                                [((t, LANES), BF16, 6), ((seq, LANES), BF16, 4), ((t, LANES), F32, 2),
                                 ((8, seq), F32, 2)] + _sweep_buffers(t, 1),
                                6 * t * t * 4),
        name="fox_attn",
    )(bq, bk, bv, bgate, fcol, frow)


def _band_bias_kernel(table_ref, o_ref):
    row = lax.broadcasted_iota(jnp.int32, (BAND_ROWS, TOEPLITZ), 0)
    qc = lax.broadcasted_iota(jnp.int32, (BAND_ROWS, BAND_KEYS), 0) // CHUNK
    kc = lax.broadcasted_iota(jnp.int32, (BAND_ROWS, BAND_KEYS), 1) // CHUNK
    for v in range(BAND_VARIANTS):
        x = jnp.broadcast_to(table_ref[v], (BAND_ROWS, TOEPLITZ))
        shift = 1
        while shift < BAND_ROWS:
            x = jnp.where((row & shift) != 0, pltpu.roll(x, shift, 1), x)
            shift *= 2
        q_chunk = qc + v * (BAND_ROWS // CHUNK)
        visible = (kc <= q_chunk) & (kc >= q_chunk - C_LEFT_CHUNKS)
        o_ref[v] = jnp.where(visible, x[:, :BAND_KEYS], NEG)


def _band_bias(rel_bias):
    j = np.arange(TOEPLITZ)
    key_minus_q = np.where(j < BAND_KEYS, j, j - TOEPLITZ)
    idx = np.stack([np.clip(v * BAND_ROWS - key_minus_q, -REL_CLIP, REL_CLIP) + REL_CLIP
                    for v in range(BAND_VARIANTS)])
    table = (rel_bias.astype(F32) * LOG2E)[:, idx][:, :, None, :]
    return pl.pallas_call(
        _band_bias_kernel,
        grid=(C_HEADS,),
        in_specs=[pl.BlockSpec((None, BAND_VARIANTS, 1, TOEPLITZ), lambda h: (h, 0, 0, 0))],
        out_specs=pl.BlockSpec((None, BAND_VARIANTS, BAND_ROWS, BAND_KEYS), lambda h: (h, 0, 0, 0)),
        out_shape=jax.ShapeDtypeStruct((C_HEADS, BAND_VARIANTS, BAND_ROWS, BAND_KEYS), F32),
        compiler_params=_params(("parallel",), [((BAND_VARIANTS, BAND_ROWS, BAND_KEYS), F32, 2)],
                                6 * BAND_ROWS * TOEPLITZ * 4),
        name="band_bias",
    )(table)


def _band_attn_kernel(q_ref, k_ref, v_ref, bias_ref, o_ref):
    i = pl.program_id(2)
    first = jnp.maximum(i - (BAND_VARIANTS - 1), 0)
    start = pl.multiple_of(first * BAND_ROWS, BAND_ROWS)
    k = k_ref[pl.ds(start, BAND_KEYS), :]
    v = v_ref[pl.ds(start, BAND_KEYS), :]
    s = lax.dot_general(q_ref[...], k, _NT, preferred_element_type=F32) + bias_ref[i - first]
    p = jnp.exp2(s - jnp.max(s, axis=-1, keepdims=True))
    denom = jnp.sum(p, axis=-1, keepdims=True)
    o = jnp.dot(p.astype(BF16), v, preferred_element_type=F32)
    o_ref[...] = (o / denom).astype(BF16)


def _band_attention(cq, ck, cv, bias, batch, seq):
    nq = seq // BAND_ROWS
    qo = pl.BlockSpec((BAND_ROWS, LANES), lambda b, h, i: (b * nq + i, h))
    kv = pl.BlockSpec((seq, LANES), lambda b, h, i: (b, h))
    return pl.pallas_call(
        _band_attn_kernel,
        grid=(batch, C_HEADS, nq),
        in_specs=[qo, kv, kv,
                  pl.BlockSpec((None, BAND_VARIANTS, BAND_ROWS, BAND_KEYS), lambda b, h, i: (h, 0, 0, 0))],
        out_specs=qo,
        out_shape=jax.ShapeDtypeStruct(cq.shape, BF16),
        compiler_params=_params(("parallel", "parallel", "arbitrary"),
                                [((BAND_ROWS, LANES), BF16, 4), ((seq, LANES), BF16, 4),
                                 ((BAND_VARIANTS, BAND_ROWS, BAND_KEYS), F32, 2)],
                                4 * BAND_ROWS * BAND_KEYS * 4),
        name="band_attn",
    )(cq, ck, cv, bias)


def _outproj_kernel(oa_ref, ob_ref, oc_ref, x_ref, g_ref, wa_ref, wb_ref, wc_ref, o_ref):
    y = jnp.dot(oa_ref[...], wa_ref[...], preferred_element_type=F32)
    y += jnp.dot(ob_ref[...], wb_ref[...], preferred_element_type=F32)
    y += jnp.dot(oc_ref[...], wc_ref[...], preferred_element_type=F32)
    o_ref[...] = x_ref[...] + _rms(y, g_ref[...])


def _outproj(oa, ob, oc, x, g, w_out):
    n, d = x.shape
    tm = min(OUT_ROWS, n)
    w = w_out.astype(BF16)
    wa, wb, wc = w[:A_WIDTH], w[A_WIDTH:A_WIDTH + B_WIDTH], w[A_WIDTH + B_WIDTH:]

    def rows(width):
        return pl.BlockSpec((tm, width), lambda i: (i, 0))

    return pl.pallas_call(
        _outproj_kernel,
        grid=(n // tm,),
        in_specs=[rows(A_WIDTH), rows(B_WIDTH), rows(C_WIDTH), rows(d), _resident((1, d)),
                  _resident(wa.shape), _resident(wb.shape), _resident(wc.shape)],
        out_specs=rows(d),
        out_shape=jax.ShapeDtypeStruct((n, d), F32),
        compiler_params=_params(("parallel",),
                                [((tm, d), F32, 4), ((tm, d), BF16, 2), (w.shape, BF16, 1)], 2 * tm * d * 4),
        name="outproj",
    )(oa, ob, oc, x, g[None, :], wa, wb, wc)


def _mem_kv_kernel(mem_ref, g_ref, w_ref, o_ref):
    u = _rms(mem_ref[...], g_ref[...]).astype(BF16)
    o_ref[...] = jnp.dot(u, w_ref[...], preferred_element_type=F32).astype(BF16)


def _mem_kv(mem2d, g, w_kv, mem_len):
    n, d = mem2d.shape
    w = w_kv.astype(BF16)
    return pl.pallas_call(
        _mem_kv_kernel,
        grid=(n // mem_len,),
        in_specs=[pl.BlockSpec((mem_len, d), lambda b: (b, 0)), _resident((1, d)), _resident(w.shape)],
        out_specs=pl.BlockSpec((mem_len, w.shape[1]), lambda b: (b, 0)),
        out_shape=jax.ShapeDtypeStruct((n, w.shape[1]), BF16),
        compiler_params=_params(("parallel",), [((mem_len, d), F32, 3), (w.shape, BF16, 1)], mem_len * d * 8),
        name="mem_kv",
    )(mem2d, g[None, :], w)


def _mem_attn_kernel(x_ref, pre_ref, post_ref, wq_ref, kv_ref, wo_ref, o_ref):
    x = x_ref[...]
    u = _rms(x, pre_ref[...]).astype(BF16)
    q = (jnp.dot(u, wq_ref[...], preferred_element_type=F32) * (HEAD_DIM ** -0.5 * LOG2E)).astype(BF16)
    heads = []
    for h in range(MEM_HEADS):
        qh = q[:, h * LANES:(h + 1) * LANES]
        kh = kv_ref[:, h * LANES:(h + 1) * LANES]
        vh = kv_ref[:, MEM_WIDTH + h * LANES:MEM_WIDTH + (h + 1) * LANES]
        s = lax.dot_general(qh, kh, _NT, preferred_element_type=F32)
        p = jnp.exp2(s - jnp.max(s, axis=-1, keepdims=True))
        denom = jnp.sum(p, axis=-1, keepdims=True)
        heads.append((jnp.dot(p.astype(BF16), vh, preferred_element_type=F32) / denom).astype(BF16))
    y = jnp.dot(jnp.concatenate(heads, axis=1), wo_ref[...], preferred_element_type=F32)
    o_ref[...] = x + _rms(y, post_ref[...])


def _mem_attention(x, pre_g, post_g, w_q, kv, w_o, seq, mem_len):
    n, d = x.shape
    tm = min(OUT_ROWS, seq)
    tiles_per_seq = seq // tm
    wq, wo = w_q.astype(BF16), w_o.astype(BF16)
    row = pl.BlockSpec((tm, d), lambda i: (i, 0))
    return pl.pallas_call(
        _mem_attn_kernel,
        grid=(n // tm,),
        in_specs=[row, _resident((1, d)), _resident((1, d)), _resident(wq.shape),
                  pl.BlockSpec((mem_len, kv.shape[1]), lambda i: (i // tiles_per_seq, 0)),
                  _resident(wo.shape)],
        out_specs=row,
        out_shape=jax.ShapeDtypeStruct((n, d), F32),
        compiler_params=_params(("parallel",),
                                [((tm, d), F32, 4), (wq.shape, BF16, 1), (wo.shape, BF16, 1),
                                 ((mem_len, kv.shape[1]), BF16, 2)], 3 * tm * d * 4),
        name="mem_attn",
    )(x, pre_g[None, :], post_g[None, :], wq, kv, wo)


def kernel(x, mem, positions, ffn1_pre_g, ffn1_post_g, ffn1_w_gate, ffn1_w_up, ffn1_w_down, mix_pre_g, mix_post_g, w_in, w_out, lam_q1, lam_k1, lam_q2, lam_k2, diff_subln_g, fox_forget_b, chunk_rel_bias, mem_pre_g, mem_post_g, mem_kv_g, w_mem_q, w_mem_kv, w_mem_o, ffn2_pre_g, ffn2_post_g, ffn2_w_gate, ffn2_w_up, ffn2_w_down):
    batch, seq, d = x.shape
    mem_len = mem.shape[1]
    depth = w_in.shape[0]
    assert seq % ATTN_TILE == 0 or seq < ATTN_TILE, seq
    assert seq % BAND_ROWS == 0 and seq >= BAND_KEYS, seq

    h = x.reshape(batch * seq, d)
    mem2d = mem.reshape(batch * mem_len, d)
    cos, sin = _rope_table(positions)
    for l in range(depth):
        lam_init = 0.8 - 0.6 * math.exp(-0.3 * l)
        h = _ffn(h, ffn1_pre_g[l], ffn1_post_g[l], ffn1_w_gate[l], ffn1_w_up[l], ffn1_w_down[l])

        aq, ak, av, bq, bk, bv, bgate, fcol, frow, cq, ck, cv = _inproj(
            h, mix_pre_g[l], cos, sin, w_in[l], fox_forget_b[l], seq)
        lam_params = jnp.pad(jnp.stack([lam_q1[l], lam_k1[l], lam_q2[l], lam_k2[l]]).astype(F32),
                             ((0, 0), (0, LANES - A_QK_DIM)))
        oa = _diff_attention(aq, ak, av, lam_params, diff_subln_g[l], batch, seq, lam_init)
        frow_heads = frow[:B_HEADS].reshape(B_HEADS, batch, 1, seq)
        ob = _fox_attention(bq, bk, bv, bgate, fcol, frow_heads, batch, seq)
        oc = _band_attention(cq, ck, cv, _band_bias(chunk_rel_bias[l]), batch, seq)
        h = _outproj(oa, ob, oc, h, mix_post_g[l], w_out[l])

        kv = _mem_kv(mem2d, mem_kv_g[l], w_mem_kv[l], mem_len)
        h = _mem_attention(h, mem_pre_g[l], mem_post_g[l], w_mem_q[l], kv, w_mem_o[l], seq, mem_len)

        h = _ffn(h, ffn2_pre_g[l], ffn2_post_g[l], ffn2_w_gate[l], ffn2_w_up[l], ffn2_w_down[l])
    return h.reshape(batch, seq, d)
```

```python
import functools
import math
from typing import NamedTuple

import numpy as np
import jax
import jax.numpy as jnp
from jax import lax
from jax.experimental import pallas as pl
from jax.experimental.pallas import tpu as pltpu

F32 = jnp.float32
BF16 = jnp.bfloat16

CHUNK = 64
HEAD_DIM = 128
A_HEADS, B_HEADS, C_HEADS = 6, 5, 5
A_QK_DIM = HEAD_DIM // 2
A_WIDTH, B_WIDTH, C_WIDTH = A_HEADS * HEAD_DIM, B_HEADS * HEAD_DIM, C_HEADS * HEAD_DIM
C_LEFT_CHUNKS = 8
REL_CLIP = 256
MEM_HEADS = 4
MEM_WIDTH = MEM_HEADS * HEAD_DIM
ROPE_THETA = 10000.0
EPS = 1e-6
NEG = -1e30
LOG2E = math.log2(math.e)

LANES = 128
V7X_VMEM_BYTES = 64 * 2**20
VMEM_COMPILER_RESERVE = 8 * 2**20

FFN_ROWS, FFN_COLS = 512, 512
PROJ_ROWS = 256
ATTN_TILE = 512
BAND_ROWS = 512
BAND_KEYS = BAND_ROWS + C_LEFT_CHUNKS * CHUNK
BAND_VARIANTS = BAND_KEYS // BAND_ROWS
TOEPLITZ = BAND_KEYS + BAND_ROWS
OUT_ROWS = 512

_NT = (((1,), (1,)), ((), ()))


def _nbytes(shape, dtype):
    return math.prod(shape) * jnp.dtype(dtype).itemsize


def _params(semantics, buffers, temp_bytes=0):
    need = sum(_nbytes(s, d) * n for s, d, n in buffers) + temp_bytes
    limit = min(need + VMEM_COMPILER_RESERVE, V7X_VMEM_BYTES - VMEM_COMPILER_RESERVE // 2)
    return pltpu.CompilerParams(dimension_semantics=semantics, vmem_limit_bytes=int(limit))


def _resident(shape):
    n = len(shape)
    return pl.BlockSpec(shape, lambda *_: (0,) * n, pipeline_mode=pl.Buffered(1))


def _rms(x, g):
    return x * lax.rsqrt(jnp.mean(x * x, axis=-1, keepdims=True) + EPS) * g


def _lane_tile(a, n):
    return a if n == 1 else jnp.concatenate([a] * n, axis=1)


def _rope_table_kernel(pos_ref, invf_ref, sign_ref, cos_ref, sin_ref):
    ang = pos_ref[...].astype(F32) * invf_ref[...]
    cos_ref[...] = jnp.cos(ang)
    sin_ref[...] = jnp.sin(ang) * sign_ref[...]


def _rope_table(positions):
    n = positions.size
    rows = math.gcd(n, 2048)
    half = A_QK_DIM // 2
    inv_freq = jnp.power(jnp.float32(ROPE_THETA), -jnp.arange(half, dtype=F32) / half)
    invf = jnp.tile(inv_freq, LANES // half)[None, :]
    sign = jnp.tile(jnp.concatenate([-jnp.ones((half,), F32), jnp.ones((half,), F32)]),
                    LANES // (2 * half))[None, :]
    row = pl.BlockSpec((rows, LANES), lambda i: (i, 0))
    return pl.pallas_call(
        _rope_table_kernel,
        grid=(n // rows,),
        in_specs=[pl.BlockSpec((rows, 1), lambda i: (i, 0)), _resident((1, LANES)), _resident((1, LANES))],
        out_specs=[row, row],
        out_shape=[jax.ShapeDtypeStruct((n, LANES), F32)] * 2,
        compiler_params=_params(("parallel",), [((rows, LANES), F32, 6)], 8 * rows * LANES * 4),
        name="rope_table",
    )(positions.reshape(n, 1), invf, sign)


def _ffn_kernel(x_ref, pre_ref, post_ref, wg_ref, wu_ref, wd_ref, o_ref, u_ref):
    f = pl.program_id(1)

    @pl.when(f == 0)
    def _():
        u_ref[...] = _rms(x_ref[...], pre_ref[...]).astype(BF16)
        o_ref[...] = jnp.zeros_like(o_ref)

    u = u_ref[...]
    g = jnp.dot(u, wg_ref[...], preferred_element_type=F32)
    v = jnp.dot(u, wu_ref[...], preferred_element_type=F32)
    h = (g * jax.nn.sigmoid(g) * v).astype(BF16)
    o_ref[...] += jnp.dot(h, wd_ref[...], preferred_element_type=F32)

    @pl.when(f == pl.num_programs(1) - 1)
    def _():
        o_ref[...] = x_ref[...] + 0.5 * _rms(o_ref[...], post_ref[...])


def _ffn(x, pre_g, post_g, w_gate, w_up, w_down):
    n, d = x.shape
    dff = w_gate.shape[1]
    tm, tf = min(FFN_ROWS, n), FFN_COLS
    row = pl.BlockSpec((tm, d), lambda i, f: (i, 0))
    return pl.pallas_call(
        _ffn_kernel,
        grid=(n // tm, dff // tf),
        in_specs=[row, _resident((1, d)), _resident((1, d)),
                  pl.BlockSpec((d, tf), lambda i, f: (0, f)),
                  pl.BlockSpec((d, tf), lambda i, f: (0, f)),
                  pl.BlockSpec((tf, d), lambda i, f: (f, 0))],
        out_specs=row,
        out_shape=jax.ShapeDtypeStruct((n, d), F32),
        scratch_shapes=[pltpu.VMEM((tm, d), BF16)],
        compiler_params=_params(("parallel", "arbitrary"),
                                [((tm, d), F32, 4), ((tm, d), BF16, 1), ((d, tf), BF16, 6)],
                                4 * tm * tf * 4),
        name="ffn",
    )(x, pre_g[None, :], post_g[None, :], w_gate.astype(BF16), w_up.astype(BF16), w_down.astype(BF16))


def _inproj_kernel(x_ref, g_ref, cos_ref, sin_ref, wa_ref, wbc_ref, bbf,
                   aq, ak, av, bq, bk, bv, bgate, fcol, frow, cq, ck, cv,
                   carry, *, tiles_per_seq):
    i = pl.program_id(0)
    tm = x_ref.shape[0]
    u = _rms(x_ref[...], g_ref[...]).astype(BF16)
    cos, sin = cos_ref[...], sin_ref[...]
    lane = lax.broadcasted_iota(jnp.int32, (tm, LANES), 1)
    first_half = (lane & (A_QK_DIM // 2)) == 0

    a = jnp.dot(u, wa_ref[...], preferred_element_type=F32)
    bc = jnp.dot(u, wbc_ref[...], preferred_element_type=F32)

    def rope_store(cols, o_ref, scale):
        for h in range(A_HEADS):
            t = cols[:, h * LANES:(h + 1) * LANES]
            partner = jnp.where(first_half, pltpu.roll(t, LANES - A_QK_DIM // 2, 1),
                                pltpu.roll(t, A_QK_DIM // 2, 1))
            o_ref[:, h * LANES:(h + 1) * LANES] = ((t * cos + partner * sin) * scale).astype(BF16)

    def bc_cols(k):
        return bc[:, k * B_WIDTH:(k + 1) * B_WIDTH]

    rope_store(a[:, :A_WIDTH], aq, A_QK_DIM ** -0.5 * LOG2E)
    rope_store(a[:, A_WIDTH:2 * A_WIDTH], ak, 1.0)
    av[...] = a[:, 2 * A_WIDTH:].astype(BF16)
    bq[...] = (bc_cols(0) * (HEAD_DIM ** -0.5 * LOG2E)).astype(BF16)
    bk[...] = bc_cols(1).astype(BF16)
    bv[...] = bc_cols(2).astype(BF16)
    bgate[...] = jax.nn.sigmoid(bc_cols(3)).astype(BF16)
    cq[...] = (bc_cols(4) * (HEAD_DIM ** -0.5 * LOG2E)).astype(BF16)
    ck[...] = bc_cols(5).astype(BF16)
    cv[...] = bc_cols(6).astype(BF16)

    z = bc[:, 7 * B_WIDTH:] + bbf[...]
    logf = jnp.minimum(z, 0.0) - jnp.log1p(jnp.exp(-jnp.abs(z)))

    @pl.when(i % tiles_per_seq == 0)
    def _():
        carry[...] = jnp.zeros_like(carry)

    r = lax.broadcasted_iota(jnp.int32, (tm, tm), 0)
    c = lax.broadcasted_iota(jnp.int32, (tm, tm), 1)
    tri = (r >= c).astype(F32)
    cum = jnp.dot(tri, logf, precision=lax.Precision.HIGHEST, preferred_element_type=F32) + carry[...]
    carry[...] = cum[tm - 1:tm, :]
    cum2 = cum * LOG2E
    fcol[...] = cum2
    frow[...] = cum2.T


def _inproj(x, g, cos, sin, w_in, forget_b, seq):
    n, d = x.shape
    tm = min(PROJ_ROWS, seq)
    assert B_WIDTH == C_WIDTH
    a_cols, b_cols = 3 * A_WIDTH, 4 * B_WIDTH
    w = w_in.astype(BF16)
    wa = w[:, :a_cols]
    wbf = jnp.pad(w[:, a_cols + b_cols:a_cols + b_cols + B_HEADS], ((0, 0), (0, LANES - B_HEADS)))
    wbc = jnp.concatenate([w[:, a_cols:a_cols + b_cols], w[:, a_cols + b_cols + B_HEADS:], wbf], axis=1)
    bbf = jnp.pad(forget_b.astype(F32), (0, LANES - B_HEADS))[None, :]
    weights = (wa, wbc, bbf)

    def rows(width):
        return pl.BlockSpec((tm, width), lambda i: (i, 0))

    out_widths = (A_WIDTH,) * 3 + (B_WIDTH,) * 4 + (LANES,) + (C_WIDTH,) * 3
    out_specs = [rows(A_WIDTH)] * 3 + [rows(B_WIDTH)] * 4 + [rows(LANES), pl.BlockSpec((LANES, tm), lambda i: (0, i))] \
        + [rows(C_WIDTH)] * 3
    out_shape = [jax.ShapeDtypeStruct((n, A_WIDTH), BF16)] * 3 + [jax.ShapeDtypeStruct((n, B_WIDTH), BF16)] * 4 \
        + [jax.ShapeDtypeStruct((n, LANES), F32), jax.ShapeDtypeStruct((LANES, n), F32)] \
        + [jax.ShapeDtypeStruct((n, C_WIDTH), BF16)] * 3
    buffers = [((tm, d), F32, 2), ((tm, LANES), F32, 8)] + [(w.shape, w.dtype, 1) for w in weights] \
        + [((tm, w), BF16, 2) for w in out_widths]
    return pl.pallas_call(
        functools.partial(_inproj_kernel, tiles_per_seq=seq // tm),
        grid=(n // tm,),
        in_specs=[rows(d), _resident((1, d)), rows(LANES), rows(LANES)] + [_resident(w.shape) for w in weights],
        out_specs=out_specs,
        out_shape=out_shape,
        scratch_shapes=[pltpu.VMEM((1, LANES), F32)],
        compiler_params=_params(("arbitrary",), buffers, 2 * tm * (wa.shape[1] + wbc.shape[1]) * 4 + tm * d * 6),
        name="inproj",
    )(x, g[None, :], cos, sin, *weights)


class _Stream(NamedTuple):
    q: jax.Array
    bias_fn: object
    mask_pen: object
    mask_last: object
    m_ref: object
    acc_ref: object
    s_refs: tuple
    p_refs: tuple
    a_refs: tuple


def _causal_sweep(pair, t, k_ref, v_ref, streams):
    ones = jnp.ones((t, LANES), BF16)
    pen_masks = tuple(st.mask_pen for st in streams)
    last_masks = tuple(st.mask_last for st in streams)
    no_masks = (None,) * len(streams)

    def scores(j):
        start = pl.multiple_of(j * t, t)
        k = k_ref[pl.ds(start, t), :]
        out = []
        for st in streams:
            s = lax.dot_general(st.q, k, _NT, preferred_element_type=F32)
            out.append(s if st.bias_fn is None else s + st.bias_fn(start))
        return tuple(out)

    def softmax(ss, masks):
        out = []
        for st, s, mask in zip(streams, ss, masks):
            if mask is not None:
                s = jnp.where(mask, s, NEG)
            m_prev = st.m_ref[...]
            m_new = jnp.maximum(m_prev, jnp.max(s, axis=-1, keepdims=True))
            st.m_ref[...] = m_new
            out.append((jnp.exp2(s - _lane_tile(m_new, t // LANES)).astype(BF16), jnp.exp2(m_prev - m_new)))
        return tuple(out)

    def accumulate(j, weights):
        start = pl.multiple_of(j * t, t)
        v_ext = jnp.concatenate([v_ref[pl.ds(start, t), :], ones], axis=1)
        for st, (p, alpha) in zip(streams, weights):
            pv = jnp.dot(p, v_ext, preferred_element_type=F32)
            st.acc_ref[...] = _lane_tile(alpha, 2) * st.acc_ref[...] + pv

    for st in streams:
        st.m_ref[...] = jnp.full(st.m_ref.shape, NEG, F32)
        st.acc_ref[...] = jnp.zeros_like(st.acc_ref)

    @pl.when(pair == 0)
    def _():
        s0 = scores(0)
        s1 = scores(1)
        accumulate(0, softmax(s0, pen_masks))
        accumulate(1, softmax(s1, last_masks))

    def put_scores(j, par):
        for st, s in zip(streams, scores(j)):
            st.s_refs[par][...] = s

    def take_scores(par):
        return tuple(st.s_refs[par][...] for st in streams)

    def put_weights(par, weights):
        for st, (p, alpha) in zip(streams, weights):
            st.p_refs[par][...] = p
            st.a_refs[par][...] = alpha

    def take_weights(par):
        return tuple((st.p_refs[par][...], st.a_refs[par][...]) for st in streams)

    def steady(j, par, masks):
        accumulate(j - 2, take_weights(par))
        put_weights(1 - par, softmax(take_scores(1 - par), masks))
        put_scores(j, par)

    @pl.when(pair >= 1)
    def _():
        put_scores(0, 0)
        put_weights(0, softmax(take_scores(0), no_masks))
        put_scores(1, 1)

        def steady_pair(n, carry):
            steady(2 + 2 * n, 0, no_masks)

            @pl.when(n >= 0)
            def _():
                steady(3 + 2 * n, 1, no_masks)

            return carry

        lax.fori_loop(0, pair - 1, steady_pair, 0)
        steady(2 * pair, 0, no_masks)

        @pl.when(pair >= 0)
        def _():
            steady(2 * pair + 1, 1, pen_masks)
            accumulate(2 * pair, take_weights(0))
            accumulate(2 * pair + 1, softmax(take_scores(1), last_masks))


def _sweep_scratch(t, n_streams):
    per_stream = [pltpu.VMEM((t, LANES), F32), pltpu.VMEM((t, 2 * LANES), F32)] \
        + [pltpu.VMEM((t, t), F32)] * 2 + [pltpu.VMEM((t, t), BF16)] * 2 + [pltpu.VMEM((t, LANES), F32)] * 2
    return per_stream * n_streams


def _stream(q, bias_fn, mask_pen, mask_last, scratch):
    m_ref, acc_ref, s0, s1, p0, p1, a0, a1 = scratch
    return _Stream(q, bias_fn, mask_pen, mask_last, m_ref, acc_ref, (s0, s1), (p0, p1), (a0, a1))


_SCRATCH_PER_STREAM = 8


def _sweep_buffers(t, n_streams):
    return [((t, LANES), F32, 5 * n_streams), ((t, t), F32, 2 * n_streams), ((t, t), BF16, 2 * n_streams)]


def _normalized(acc_ref):
    return acc_ref[:, :LANES] / acc_ref[:, LANES:]


def _diff_attn_kernel(q_ref, k_ref, v_ref, lam_ref, g_ref, o_ref, *scratch, lam_init):
    t = q_ref.shape[0] // 2
    lane = lax.broadcasted_iota(jnp.int32, (t, LANES), 1)
    row = lax.broadcasted_iota(jnp.int32, (t, t), 0)
    col = lax.broadcasted_iota(jnp.int32, (t, t), 1)
    diag = (col // CHUNK) <= (row // CHUNK)
    hidden = jnp.zeros((t, t), jnp.bool_)
    streams = []
    for half in range(2):
        q = q_ref[half * t:(half + 1) * t, :].astype(F32)
        pen, last = (diag, hidden) if half == 0 else (None, diag)
        for comp in range(2):
            keep = (lane < A_QK_DIM) if comp == 0 else (lane >= A_QK_DIM)
            qc = jnp.where(keep, q, 0.0).astype(BF16)
            n = len(streams)
            streams.append(_stream(qc, None, pen, last,
                                   scratch[n * _SCRATCH_PER_STREAM:(n + 1) * _SCRATCH_PER_STREAM]))
    _causal_sweep(pl.program_id(2), t, k_ref, v_ref, tuple(streams))

    lp = lam_ref[...]
    lam = (jnp.exp(jnp.sum(lp[0:1] * lp[1:2], axis=-1, keepdims=True))
           - jnp.exp(jnp.sum(lp[2:3] * lp[3:4], axis=-1, keepdims=True)) + lam_init)
    for half in range(2):
        o = _normalized(streams[2 * half].acc_ref) - lam * _normalized(streams[2 * half + 1].acc_ref)
        o_ref[half * t:(half + 1) * t, :] = (_rms(o, g_ref[...]) * (1.0 - lam_init)).astype(BF16)


def _diff_attention(aq, ak, av, lam_params, subln_g, batch, seq, lam_init):
    t = ATTN_TILE
    steps = seq // (2 * t)
    qo = pl.BlockSpec((2 * t, LANES), lambda b, h, i: (b * steps + i, h))
    kv = pl.BlockSpec((seq, LANES), lambda b, h, i: (b, h))
    return pl.pallas_call(
        functools.partial(_diff_attn_kernel, lam_init=lam_init),
        grid=(batch, A_HEADS, steps),
        in_specs=[qo, kv, kv, _resident((4, LANES)), _resident((1, LANES))],
        out_specs=qo,
        out_shape=jax.ShapeDtypeStruct(aq.shape, BF16),
        scratch_shapes=_sweep_scratch(t, 4),
        compiler_params=_params(("parallel", "parallel", "arbitrary"),
                                [((2 * t, LANES), BF16, 4), ((seq, LANES), BF16, 4)] + _sweep_buffers(t, 4),
                                8 * t * t * 4),
        name="diff_attn",
    )(aq, ak, av, lam_params, subln_g[None, :])


def _fox_attn_kernel(q_ref, k_ref, v_ref, gate_ref, fcol_ref, frow_ref, o_ref, *scratch):
    h = pl.program_id(1)
    t = q_ref.shape[0] // 2
    lane = lax.broadcasted_iota(jnp.int32, (2 * t, LANES), 1)
    fq = jnp.sum(jnp.where(lane == h, fcol_ref[...], 0.0), axis=-1, keepdims=True)
    row = lax.broadcasted_iota(jnp.int32, (t, t), 0)
    col = lax.broadcasted_iota(jnp.int32, (t, t), 1)
    diag = col <= row
    hidden = jnp.zeros((t, t), jnp.bool_)

    def forget_bias(fq_half):
        return lambda start: fq_half - frow_ref[:, pl.ds(start, t)]

    streams = (
        _stream(q_ref[:t, :], forget_bias(fq[:t]), diag, hidden, scratch[:_SCRATCH_PER_STREAM]),
        _stream(q_ref[t:, :], forget_bias(fq[t:]), None, diag, scratch[_SCRATCH_PER_STREAM:]),
    )
    _causal_sweep(pl.program_id(2), t, k_ref, v_ref, streams)
    for half, st in enumerate(streams):
        rows = slice(half * t, (half + 1) * t)
        o_ref[rows, :] = (_normalized(st.acc_ref) * gate_ref[rows, :].astype(F32)).astype(BF16)


def _fox_attention(bq, bk, bv, bgate, fcol, frow, batch, seq):
    t = ATTN_TILE
    steps = seq // (2 * t)
    qo = pl.BlockSpec((2 * t, LANES), lambda b, h, i: (b * steps + i, h))
    kv = pl.BlockSpec((seq, LANES), lambda b, h, i: (b, h))
    return pl.pallas_call(
        _fox_attn_kernel,
        grid=(batch, B_HEADS, steps),
        in_specs=[qo, kv, kv, qo,
                  pl.BlockSpec((2 * t, LANES), lambda b, h, i: (b * steps + i, 0)),
                  pl.BlockSpec((None, None, 1, seq), lambda b, h, i: (h, b, 0, 0))],
        out_specs=qo,
        out_shape=jax.ShapeDtypeStruct(bq.shape, BF16),
        scratch_shapes=_sweep_scratch(t, 2),
        compiler_params=_params(("parallel", "parallel", "arbitrary"),
                                [((2 * t, LANES), BF16, 6), ((seq, LANES), BF16, 4), ((2 * t, LANES), F32, 2),
                                 ((8, seq), F32, 2)] + _sweep_buffers(t, 2),
                                8 * t * t * 4),
        name="fox_attn",
    )(bq, bk, bv, bgate, fcol, frow)


def _band_bias_kernel(table_ref, o_ref):
    row = lax.broadcasted_iota(jnp.int32, (BAND_ROWS, TOEPLITZ), 0)
    qc = lax.broadcasted_iota(jnp.int32, (BAND_ROWS, BAND_KEYS), 0) // CHUNK
    kc = lax.broadcasted_iota(jnp.int32, (BAND_ROWS, BAND_KEYS), 1) // CHUNK
    for v in range(BAND_VARIANTS):
        x = jnp.broadcast_to(table_ref[v], (BAND_ROWS, TOEPLITZ))
        shift = 1
        while shift < BAND_ROWS:
            x = jnp.where((row & shift) != 0, pltpu.roll(x, shift, 1), x)
            shift *= 2
        q_chunk = qc + v * (BAND_ROWS // CHUNK)
        visible = (kc <= q_chunk) & (kc >= q_chunk - C_LEFT_CHUNKS)
        o_ref[v] = jnp.where(visible, x[:, :BAND_KEYS], NEG)


def _band_bias(rel_bias):
    j = np.arange(TOEPLITZ)
    key_minus_q = np.where(j < BAND_KEYS, j, j - TOEPLITZ)
    idx = np.stack([np.clip(v * BAND_ROWS - key_minus_q, -REL_CLIP, REL_CLIP) + REL_CLIP
                    for v in range(BAND_VARIANTS)])
    table = (rel_bias.astype(F32) * LOG2E)[:, idx][:, :, None, :]
    return pl.pallas_call(
        _band_bias_kernel,
        grid=(C_HEADS,),
        in_specs=[pl.BlockSpec((None, BAND_VARIANTS, 1, TOEPLITZ), lambda h: (h, 0, 0, 0))],
        out_specs=pl.BlockSpec((None, BAND_VARIANTS, BAND_ROWS, BAND_KEYS), lambda h: (h, 0, 0, 0)),
        out_shape=jax.ShapeDtypeStruct((C_HEADS, BAND_VARIANTS, BAND_ROWS, BAND_KEYS), F32),
        compiler_params=_params(("parallel",), [((BAND_VARIANTS, BAND_ROWS, BAND_KEYS), F32, 2)],
                                6 * BAND_ROWS * TOEPLITZ * 4),
        name="band_bias",
    )(table)


def _band_attn_kernel(q_ref, k_ref, v_ref, bias_ref, o_ref):
    i = pl.program_id(2)
    first = jnp.maximum(i - (BAND_VARIANTS - 1), 0)
    start = pl.multiple_of(first * BAND_ROWS, BAND_ROWS)
    k = k_ref[pl.ds(start, BAND_KEYS), :]
    v = v_ref[pl.ds(start, BAND_KEYS), :]
    s = lax.dot_general(q_ref[...], k, _NT, preferred_element_type=F32) + bias_ref[i - first]
    p = jnp.exp2(s - jnp.max(s, axis=-1, keepdims=True))
    denom = jnp.sum(p, axis=-1, keepdims=True)
    o = jnp.dot(p.astype(BF16), v, preferred_element_type=F32)
    o_ref[...] = (o / denom).astype(BF16)


def _band_attention(cq, ck, cv, bias, batch, seq):
    nq = seq // BAND_ROWS
    qo = pl.BlockSpec((BAND_ROWS, LANES), lambda b, h, i: (b * nq + i, h))
    kv = pl.BlockSpec((seq, LANES), lambda b, h, i: (b, h))
    return pl.pallas_call(
        _band_attn_kernel,
        grid=(batch, C_HEADS, nq),
        in_specs=[qo, kv, kv,
                  pl.BlockSpec((None, BAND_VARIANTS, BAND_ROWS, BAND_KEYS), lambda b, h, i: (h, 0, 0, 0))],
        out_specs=qo,
        out_shape=jax.ShapeDtypeStruct(cq.shape, BF16),
        compiler_params=_params(("parallel", "parallel", "arbitrary"),
                                [((BAND_ROWS, LANES), BF16, 4), ((seq, LANES), BF16, 4),
                                 ((BAND_VARIANTS, BAND_ROWS, BAND_KEYS), F32, 2)],
                                4 * BAND_ROWS * BAND_KEYS * 4),
        name="band_attn",
    )(cq, ck, cv, bias)


def _outproj_kernel(oa_ref, ob_ref, oc_ref, x_ref, g_ref, w_ref, o_ref):
    o = jnp.concatenate([oa_ref[...], ob_ref[...], oc_ref[...]], axis=1)
    y = jnp.dot(o, w_ref[...], preferred_element_type=F32)
    o_ref[...] = x_ref[...] + _rms(y, g_ref[...])


def _outproj(oa, ob, oc, x, g, w_out):
    n, d = x.shape
    tm = min(OUT_ROWS, n)
    w = w_out.astype(BF16)

    def rows(width):
        return pl.BlockSpec((tm, width), lambda i: (i, 0))

    return pl.pallas_call(
        _outproj_kernel,
        grid=(n // tm,),
        in_specs=[rows(A_WIDTH), rows(B_WIDTH), rows(C_WIDTH), rows(d), _resident((1, d)), _resident(w.shape)],
        out_specs=rows(d),
        out_shape=jax.ShapeDtypeStruct((n, d), F32),
        compiler_params=_params(("parallel",),
                                [((tm, d), F32, 4), ((tm, d), BF16, 3), (w.shape, BF16, 1)], 2 * tm * d * 4),
        name="outproj",
    )(oa, ob, oc, x, g[None, :], w)


def _mem_kv_kernel(mem_ref, g_ref, w_ref, o_ref):
    u = _rms(mem_ref[...], g_ref[...]).astype(BF16)
    o_ref[...] = jnp.dot(u, w_ref[...], preferred_element_type=F32).astype(BF16)


def _mem_kv(mem2d, g, w_kv, mem_len):
    n, d = mem2d.shape
    w = w_kv.astype(BF16)
    return pl.pallas_call(
        _mem_kv_kernel,
        grid=(n // mem_len,),
        in_specs=[pl.BlockSpec((mem_len, d), lambda b: (b, 0)), _resident((1, d)), _resident(w.shape)],
        out_specs=pl.BlockSpec((mem_len, w.shape[1]), lambda b: (b, 0)),
        out_shape=jax.ShapeDtypeStruct((n, w.shape[1]), BF16),
        compiler_params=_params(("parallel",), [((mem_len, d), F32, 3), (w.shape, BF16, 1)], mem_len * d * 8),
        name="mem_kv",
    )(mem2d, g[None, :], w)


def _mem_attn_kernel(x_ref, pre_ref, post_ref, wq_ref, kv_ref, wo_ref, o_ref):
    x = x_ref[...]
    u = _rms(x, pre_ref[...]).astype(BF16)
    q = (jnp.dot(u, wq_ref[...], preferred_element_type=F32) * (HEAD_DIM ** -0.5 * LOG2E)).astype(BF16)
    heads = []
    for h in range(MEM_HEADS):
        qh = q[:, h * LANES:(h + 1) * LANES]
        kh = kv_ref[:, h * LANES:(h + 1) * LANES]
        vh = kv_ref[:, MEM_WIDTH + h * LANES:MEM_WIDTH + (h + 1) * LANES]
        s = lax.dot_general(qh, kh, _NT, preferred_element_type=F32)
        p = jnp.exp2(s - jnp.max(s, axis=-1, keepdims=True))
        denom = jnp.sum(p, axis=-1, keepdims=True)
        heads.append((jnp.dot(p.astype(BF16), vh, preferred_element_type=F32) / denom).astype(BF16))
    y = jnp.dot(jnp.concatenate(heads, axis=1), wo_ref[...], preferred_element_type=F32)
    o_ref[...] = x + _rms(y, post_ref[...])


def _mem_attention(x, pre_g, post_g, w_q, kv, w_o, seq, mem_len):
    n, d = x.shape
    tm = min(OUT_ROWS, seq)
    tiles_per_seq = seq // tm
    wq, wo = w_q.astype(BF16), w_o.astype(BF16)
    row = pl.BlockSpec((tm, d), lambda i: (i, 0))
    return pl.pallas_call(
        _mem_attn_kernel,
        grid=(n // tm,),
        in_specs=[row, _resident((1, d)), _resident((1, d)), _resident(wq.shape),
                  pl.BlockSpec((mem_len, kv.shape[1]), lambda i: (i // tiles_per_seq, 0)),
                  _resident(wo.shape)],
        out_specs=row,
        out_shape=jax.ShapeDtypeStruct((n, d), F32),
        compiler_params=_params(("parallel",),
                                [((tm, d), F32, 4), (wq.shape, BF16, 1), (wo.shape, BF16, 1),
                                 ((mem_len, kv.shape[1]), BF16, 2)], 3 * tm * d * 4),
        name="mem_attn",
    )(x, pre_g[None, :], post_g[None, :], wq, kv, wo)


def kernel(x, mem, positions, ffn1_pre_g, ffn1_post_g, ffn1_w_gate, ffn1_w_up, ffn1_w_down, mix_pre_g, mix_post_g, w_in, w_out, lam_q1, lam_k1, lam_q2, lam_k2, diff_subln_g, fox_forget_b, chunk_rel_bias, mem_pre_g, mem_post_g, mem_kv_g, w_mem_q, w_mem_kv, w_mem_o, ffn2_pre_g, ffn2_post_g, ffn2_w_gate, ffn2_w_up, ffn2_w_down):
    batch, seq, d = x.shape
    mem_len = mem.shape[1]
    depth = w_in.shape[0]
    assert seq % (2 * ATTN_TILE) == 0, seq
    assert seq % BAND_ROWS == 0 and seq >= BAND_KEYS, seq

    h = x.reshape(batch * seq, d)
    mem2d = mem.reshape(batch * mem_len, d)
    cos, sin = _rope_table(positions)
    for l in range(depth):
        lam_init = 0.8 - 0.6 * math.exp(-0.3 * l)
        h = _ffn(h, ffn1_pre_g[l], ffn1_post_g[l], ffn1_w_gate[l], ffn1_w_up[l], ffn1_w_down[l])

        aq, ak, av, bq, bk, bv, bgate, fcol, frow, cq, ck, cv = _inproj(
            h, mix_pre_g[l], cos, sin, w_in[l], fox_forget_b[l], seq)
        lam_params = jnp.pad(jnp.stack([lam_q1[l], lam_k1[l], lam_q2[l], lam_k2[l]]).astype(F32),
                             ((0, 0), (0, LANES - A_QK_DIM)))
        oa = _diff_attention(aq, ak, av, lam_params, diff_subln_g[l], batch, seq, lam_init)
        frow_heads = frow[:B_HEADS].reshape(B_HEADS, batch, 1, seq)
        ob = _fox_attention(bq, bk, bv, bgate, fcol, frow_heads, batch, seq)
        oc = _band_attention(cq, ck, cv, _band_bias(chunk_rel_bias[l]), batch, seq)
        h = _outproj(oa, ob, oc, h, mix_post_g[l], w_out[l])

        kv = _mem_kv(mem2d, mem_kv_g[l], w_mem_kv[l], mem_len)
        h = _mem_attention(h, mem_pre_g[l], mem_post_g[l], w_mem_q[l], kv, w_mem_o[l], seq, mem_len)

        h = _ffn(h, ffn2_pre_g[l], ffn2_post_g[l], ffn2_w_gate[l], ffn2_w_up[l], ffn2_w_down[l])
    return h.reshape(batch, seq, d)
```
